```python
import math
import jax, jax.numpy as jnp
from jax import lax
import numpy as np

D_MODEL = 2048
BATCH = 2
SEQ = 4096
DEPTH = 2
DEC_BATCH = 128
DEC_SEQ = 8
PAST_LEN = 2048
PAGE_SIZE = 128

HEAD_DIM = 128
DN_HEADS = 8
DN_WIDTH = DN_HEADS * HEAD_DIM
DN_CONV = 4
DN_CHUNK = 64
SB_HEADS = 8
SB_WIDTH = SB_HEADS * HEAD_DIM
SB_BLOCK = 128
AB_SPLITS = (3 * DN_WIDTH, DN_WIDTH, DN_HEADS, DN_HEADS, SB_WIDTH, SB_WIDTH, SB_WIDTH)
AB_IN = sum(AB_SPLITS)
D_RNN = D_MODEL
LRU_BLOCKS = 8
LRU_BLOCK = D_RNN // LRU_BLOCKS
LRU_CONV = 4
LRU_C = 8.0
N_MEM = 256
X_HEADS = 4
X_WIDTH = X_HEADS * HEAD_DIM
D_FF = ((8 * D_MODEL // 3 + 127) // 128) * 128
FFN_CONV = 3
N_EVEN = (DEPTH + 1) // 2
N_ODD = DEPTH // 2
DEEPNORM_ALPHA = (2.0 * DEPTH) ** 0.25
DEEPNORM_BETA = (8.0 * DEPTH) ** -0.25
LN_EPS = 1e-5
NORM_EPS = 1e-6

kernel_name = 'hybrid_deltanet_stickbreak_rglru_step'


def split_last(x, sizes):
    idx, s = [], 0
    for n in sizes[:-1]:
        s += n
        idx.append(s)
    return jnp.split(x, idx, axis=-1)


def post_norm(x, s, g, b):
    h = (DEEPNORM_ALPHA * x + s).astype(jnp.float32)
    hc = h - jnp.mean(h, -1, keepdims=True)
    var = jnp.mean(hc * hc, -1, keepdims=True)
    return (hc * lax.rsqrt(var + LN_EPS) * g + b).astype(x.dtype)


def l2norm(x):
    x = x.astype(jnp.float32)
    return x * lax.rsqrt(jnp.sum(x * x, -1, keepdims=True) + NORM_EPS)


def causal_dwconv(u, buf, w, b=None):
    width = w.shape[0]
    t = u.shape[1]
    ext = jnp.concatenate([buf.astype(u.dtype), u], axis=1)
    y = ext[:, width - 1:width - 1 + t] * w[width - 1]
    for j in range(width - 1):
        y = y + ext[:, j:j + t] * w[j]
    if b is not None:
        y = y + b
    return y, ext[:, t:]


def gated_delta_rule(q, k, v, g, beta, s0):
    bsz, t, h, _ = q.shape
    dv = v.shape[-1]
    c = t if t <= DN_CHUNK else math.gcd(t, DN_CHUNK)
    n = t // c
    f32 = jnp.float32

    def to_chunks(a):
        a = a.astype(f32).reshape((bsz, n, c, h) + a.shape[3:])
        return jnp.moveaxis(a, 3, 2)

    q, k, v, g, beta = (to_chunks(a) for a in (q, k, v, g, beta))
    gc = jnp.cumsum(g, axis=-1)
    incl = jnp.tril(jnp.ones((c, c), bool))
    strict = jnp.tril(jnp.ones((c, c), bool), -1)
    decay = jnp.exp(jnp.where(incl, gc[..., :, None] - gc[..., None, :], -jnp.inf))
    kb = k * beta[..., None]
    m = jnp.where(strict, jnp.einsum('bnhid,bnhjd->bnhij', kb, k) * decay, 0.0)
    eye = jnp.eye(c, dtype=f32)
    tinv = lax.linalg.triangular_solve(eye + m, jnp.broadcast_to(eye, m.shape), left_side=True, lower=True)
    u = jnp.einsum('bnhij,bnhjd->bnhid', tinv, v * beta[..., None])
    w = jnp.einsum('bnhij,bnhjd->bnhid', tinv, kb * jnp.exp(gc)[..., None])
    attn = jnp.einsum('bnhid,bnhjd->bnhij', q, k) * decay
    qg = q * jnp.exp(gc)[..., None]
    kg = k * jnp.exp(gc[..., -1:] - gc)[..., None]
    g_last = jnp.exp(gc[..., -1])

    def step(s, xs):
        qg_i, kg_i, u_i, w_i, attn_i, gl_i = xs
        v_new = u_i - jnp.einsum('bhcd,bhde->bhce', w_i, s)
        o = jnp.einsum('bhcd,bhde->bhce', qg_i, s) + jnp.einsum('bhij,bhje->bhie', attn_i, v_new)
        s = s * gl_i[..., None, None] + jnp.einsum('bhcd,bhce->bhde', kg_i, v_new)
        return s, o

    xs = tuple(jnp.moveaxis(a, 1, 0) for a in (qg, kg, u, w, attn, g_last))
    s, o = lax.scan(step, s0.astype(f32), xs)
    o = jnp.moveaxis(jnp.moveaxis(o, 0, 1), 2, 3).reshape(bsz, t, h, dv)
    return o, s


def stick_breaking(q, k, v, offset, bias):
    t = q.shape[1]
    d = q.shape[-1]
    bias = bias.astype(jnp.float32)[None, :, None, None]
    outs = []
    for start in range(0, t, SB_BLOCK):
        end = min(start + SB_BLOCK, t)
        n_keys = offset + end
        z = jnp.einsum('bqhd,bkhd->bhqk', q[:, start:end], k[:, :n_keys],
                       preferred_element_type=jnp.float32) * d ** -0.5 + bias
        qpos = offset + start + jnp.arange(end - start)
        kpos = jnp.arange(n_keys)
        mask = kpos[None, :] < qpos[:, None]
        log_keep = jnp.where(mask, jax.nn.log_sigmoid(-z), 0.0)
        log_w = jax.nn.log_sigmoid(z) + lax.cumsum(log_keep, axis=3, reverse=True) - log_keep
        a = jnp.where(mask, jnp.exp(log_w), 0.0)
        outs.append(jnp.einsum('bhqk,bkhd->bqhd', a.astype(v.dtype), v[:, :n_keys]))
    return jnp.concatenate(outs, axis=1)


def ab_mixer(x, conv_buf, s0, past_k, past_v, w_in, conv_w, a_log, dt_bias, norm_w, sb_bias, w_out):
    bsz, t, _ = x.shape
    qkv, gate, a, b, q_sb, k_sb, v_sb = split_last(x @ w_in, AB_SPLITS)
    qkv, new_buf = causal_dwconv(qkv, conv_buf, conv_w)
    qkv = jax.nn.silu(qkv).reshape(bsz, t, 3, DN_HEADS, HEAD_DIM)
    q = l2norm(qkv[:, :, 0]) * HEAD_DIM ** -0.5
    k = l2norm(qkv[:, :, 1])
    v = qkv[:, :, 2]
    g = -jnp.exp(a_log.astype(jnp.float32)) * jax.nn.softplus(a.astype(jnp.float32) + dt_bias)
    beta = jax.nn.sigmoid(b.astype(jnp.float32))
    o_dn, s_new = gated_delta_rule(q, k, v, g, beta, s0)
    o_dn = o_dn * lax.rsqrt(jnp.mean(o_dn * o_dn, -1, keepdims=True) + NORM_EPS) * norm_w
    o_dn = o_dn * jax.nn.silu(gate.reshape(bsz, t, DN_HEADS, HEAD_DIM).astype(jnp.float32))
    k_sb = k_sb.reshape(bsz, t, SB_HEADS, HEAD_DIM)
    v_sb = v_sb.reshape(bsz, t, SB_HEADS, HEAD_DIM)
    k_all = jnp.concatenate([past_k.astype(x.dtype), k_sb], axis=1)
    v_all = jnp.concatenate([past_v.astype(x.dtype), v_sb], axis=1)
    o_sb = stick_breaking(q_sb.reshape(bsz, t, SB_HEADS, HEAD_DIM), k_all, v_all, past_k.shape[1], sb_bias)
    o = jnp.concatenate([o_dn.reshape(bsz, t, DN_WIDTH).astype(x.dtype),
                         o_sb.reshape(bsz, t, SB_WIDTH).astype(x.dtype)], axis=-1) @ w_out
    return o, new_buf, s_new.astype(x.dtype), k_sb, v_sb


def rglru_mixer(x, conv_buf, h0, w_in, conv_w, conv_b, w_r, b_r, w_i, b_i, lam, w_out):
    bsz, t, _ = x.shape
    f32 = jnp.float32
    y_br, x_br = jnp.split(x @ w_in, 2, axis=-1)
    xc, new_buf = causal_dwconv(x_br, conv_buf, conv_w, conv_b)
    xb = xc.reshape(bsz, t, LRU_BLOCKS, LRU_BLOCK)
    r = jax.nn.sigmoid((jnp.einsum('btnc,ncd->btnd', xb, w_r).reshape(bsz, t, D_RNN) + b_r).astype(f32))
    i = jax.nn.sigmoid((jnp.einsum('btnc,ncd->btnd', xb, w_i).reshape(bsz, t, D_RNN) + b_i).astype(f32))
    log_a = LRU_C * r * jax.nn.log_sigmoid(lam.astype(f32))
    a = jnp.exp(log_a)
    bt = jnp.sqrt(-jnp.expm1(2.0 * log_a)) * (i * xc.astype(f32))
    bt = bt.at[:, 0].add(a[:, 0] * h0.astype(f32))

    def combine(left, right):
        return (left[0] * right[0], right[0] * left[1] + right[1])

    _, h = lax.associative_scan(combine, (a, bt), axis=1)
    out = (jax.nn.gelu(y_br.astype(f32)) * h).astype(x.dtype) @ w_out
    return out, new_buf, h[:, -1].astype(x.dtype)


def mem_xattn(x, mk, mv, w_q, w_o):
    bsz, t, _ = x.shape
    q = (x @ w_q).reshape(bsz, t, X_HEADS, HEAD_DIM)
    s = jnp.einsum('bthd,bmhd->bhtm', q, mk, preferred_element_type=jnp.float32) * HEAD_DIM ** -0.5
    p = jax.nn.softmax(s, axis=-1)
    o = jnp.einsum('bhtm,bmhd->bthd', p.astype(mv.dtype), mv)
    return o.reshape(bsz, t, X_WIDTH) @ w_o


def conv_ffn(x, buf, w_up, conv_w, conv_b, w_down):
    u, new_buf = causal_dwconv(x @ w_up, buf, conv_w, conv_b)
    gate, val = jnp.split(u, 2, axis=-1)
    return (jax.nn.silu(gate) * val) @ w_down, new_buf


def setup_inputs(seed: int = 0) -> dict:
    key = jax.random.key(seed)
    ks = iter(jax.random.split(key, 64))
    f32 = jnp.float32

    def nrm(shape, scale=1.0):
        return jax.random.normal(next(ks), shape, f32) * scale

    def unif(shape, lo, hi):
        return jax.random.uniform(next(ks), shape, f32, lo, hi)

    n_pages = PAST_LEN // PAGE_SIZE
    n_used = DEC_BATCH * n_pages
    n_pool = (5 * n_used + 3) // 4
    page_table = jax.random.permutation(next(ks), n_pool)[:n_used].reshape(DEC_BATCH, n_pages).astype(jnp.int32)
    dt = jnp.exp(unif((N_EVEN, DN_HEADS), math.log(1e-3), math.log(1e-1)))
    lru_s = unif((N_ODD, D_RNN), 0.9, 0.999) ** (1.0 / LRU_C)
    return {
        'x_prompt': nrm((BATCH, SEQ, D_MODEL)),
        'x_sample': nrm((DEC_BATCH, DEC_SEQ, D_MODEL)),
        'mem_prompt': nrm((BATCH, N_MEM, D_MODEL)),
        'cache_sb_k': nrm((N_EVEN, n_pool, PAGE_SIZE, SB_HEADS, HEAD_DIM)),
        'cache_sb_v': nrm((N_EVEN, n_pool, PAGE_SIZE, SB_HEADS, HEAD_DIM)),
        'cache_mem_k': nrm((DEPTH, DEC_BATCH, N_MEM, X_HEADS, HEAD_DIM)),
        'cache_mem_v': nrm((DEPTH, DEC_BATCH, N_MEM, X_HEADS, HEAD_DIM)),
        'state_dn_conv': nrm((N_EVEN, DEC_BATCH, DN_CONV - 1, 3 * DN_WIDTH)),
        'state_dn_S': nrm((N_EVEN, DEC_BATCH, DN_HEADS, HEAD_DIM, HEAD_DIM), 0.3),
        'state_lru_conv': nrm((N_ODD, DEC_BATCH, LRU_CONV - 1, D_RNN)),
        'state_lru_h': nrm((N_ODD, DEC_BATCH, D_RNN), 0.5),
        'state_ffn_conv': nrm((DEPTH, DEC_BATCH, FFN_CONV - 1, 2 * D_FF)),
        'page_table': page_table,
        'ab_w_in': nrm((N_EVEN, D_MODEL, AB_IN), D_MODEL ** -0.5),
        'ab_conv_w': nrm((N_EVEN, DN_CONV, 3 * DN_WIDTH), DN_CONV ** -0.5),
        'dn_A_log': jnp.log(unif((N_EVEN, DN_HEADS), 1.0, 16.0)),
        'dn_dt_bias': dt + jnp.log(-jnp.expm1(-dt)),
        'dn_norm_w': 1.0 + nrm((N_EVEN, HEAD_DIM), 0.02),
        'sb_bias': unif((N_EVEN, SB_HEADS), -7.5, -6.5),
        'ab_w_out': nrm((N_EVEN, DN_WIDTH + SB_WIDTH, D_MODEL), (DN_WIDTH + SB_WIDTH) ** -0.5 * DEEPNORM_BETA),
        'c_w_in': nrm((N_ODD, D_MODEL, 2 * D_RNN), D_MODEL ** -0.5),
        'lru_conv_w': nrm((N_ODD, LRU_CONV, D_RNN), LRU_CONV ** -0.5),
        'lru_conv_b': nrm((N_ODD, D_RNN), 0.02),
        'lru_w_r': nrm((N_ODD, LRU_BLOCKS, LRU_BLOCK, LRU_BLOCK), LRU_BLOCK ** -0.5),
        'lru_b_r': nrm((N_ODD, D_RNN), 0.02),
        'lru_w_i': nrm((N_ODD, LRU_BLOCKS, LRU_BLOCK, LRU_BLOCK), LRU_BLOCK ** -0.5),
        'lru_b_i': nrm((N_ODD, D_RNN), 0.02),
        'lru_lambda': jnp.log(lru_s) - jnp.log1p(-lru_s),
        'c_w_out': nrm((N_ODD, D_RNN, D_MODEL), D_RNN ** -0.5 * DEEPNORM_BETA),
        'xa_w_q': nrm((DEPTH, D_MODEL, X_WIDTH), D_MODEL ** -0.5),
        'xa_w_k': nrm((DEPTH, D_MODEL, X_WIDTH), D_MODEL ** -0.5),
        'xa_w_v': nrm((DEPTH, D_MODEL, X_WIDTH), D_MODEL ** -0.5),
        'xa_w_o': nrm((DEPTH, X_WIDTH, D_MODEL), X_WIDTH ** -0.5 * DEEPNORM_BETA),
        'ffn_w_up': nrm((DEPTH, D_MODEL, 2 * D_FF), D_MODEL ** -0.5),
        'ffn_conv_w': nrm((DEPTH, FFN_CONV, 2 * D_FF), FFN_CONV ** -0.5),
        'ffn_conv_b': nrm((DEPTH, 2 * D_FF), 0.02),
        'ffn_w_down': nrm((DEPTH, D_FF, D_MODEL), D_FF ** -0.5 * DEEPNORM_BETA),
        'ln_g': 1.0 + nrm((DEPTH, 3, D_MODEL), 0.02),
        'ln_b': nrm((DEPTH, 3, D_MODEL), 0.02),
    }


def reference(x_prompt, x_sample, mem_prompt, cache_sb_k, cache_sb_v, cache_mem_k, cache_mem_v,
              state_dn_conv, state_dn_S, state_lru_conv, state_lru_h, state_ffn_conv, page_table,
              ab_w_in, ab_conv_w, dn_A_log, dn_dt_bias, dn_norm_w, sb_bias, ab_w_out,
              c_w_in, lru_conv_w, lru_conv_b, lru_w_r, lru_b_r, lru_w_i, lru_b_i, lru_lambda, c_w_out,
              xa_w_q, xa_w_k, xa_w_v, xa_w_o,
              ffn_w_up, ffn_conv_w, ffn_conv_b, ffn_w_down, ln_g, ln_b):
    bp = x_prompt.shape[0]
    bs = x_sample.shape[0]
    n_pages = page_table.shape[1]
    past_len = n_pages * PAGE_SIZE
    xp, xs = x_prompt, x_sample
    dt = xp.dtype
    p_sb_k, p_sb_v, p_dn_conv, p_dn_S, p_lru_conv, p_lru_h = [], [], [], [], [], []
    p_mem_k, p_mem_v, p_ffn_conv = [], [], []
    s_sb_k, s_sb_v, s_dn_conv, s_dn_S, s_lru_conv, s_lru_h, s_ffn_conv = [], [], [], [], [], [], []
    for layer in range(DEPTH):
        if layer % 2 == 0:
            e = layer // 2
            wts = (ab_w_in[e], ab_conv_w[e], dn_A_log[e], dn_dt_bias[e], dn_norm_w[e], sb_bias[e], ab_w_out[e])
            empty = jnp.zeros((bp, 0, SB_HEADS, HEAD_DIM), dt)
            mp, cb, st, k_new, v_new = ab_mixer(
                xp, jnp.zeros((bp, DN_CONV - 1, 3 * DN_WIDTH), dt),
                jnp.zeros((bp, DN_HEADS, HEAD_DIM, HEAD_DIM), jnp.float32), empty, empty, *wts)
            p_dn_conv.append(cb); p_dn_S.append(st); p_sb_k.append(k_new); p_sb_v.append(v_new)
            past_k = jnp.take(cache_sb_k[e], page_table, axis=0).reshape(bs, past_len, SB_HEADS, HEAD_DIM)
            past_v = jnp.take(cache_sb_v[e], page_table, axis=0).reshape(bs, past_len, SB_HEADS, HEAD_DIM)
            ms, cb, st, k_new, v_new = ab_mixer(xs, state_dn_conv[e], state_dn_S[e], past_k, past_v, *wts)
            s_dn_conv.append(cb); s_dn_S.append(st); s_sb_k.append(k_new); s_sb_v.append(v_new)
        else:
            o = layer // 2
            wts = (c_w_in[o], lru_conv_w[o], lru_conv_b[o], lru_w_r[o], lru_b_r[o],
                   lru_w_i[o], lru_b_i[o], lru_lambda[o], c_w_out[o])
            mp, cb, hl = rglru_mixer(xp, jnp.zeros((bp, LRU_CONV - 1, D_RNN), dt),
                                     jnp.zeros((bp, D_RNN), dt), *wts)
            p_lru_conv.append(cb); p_lru_h.append(hl)
            ms, cb, hl = rglru_mixer(xs, state_lru_conv[o], state_lru_h[o], *wts)
            s_lru_conv.append(cb); s_lru_h.append(hl)
        xp = post_norm(xp, mp, ln_g[layer, 0], ln_b[layer, 0])
        xs = post_norm(xs, ms, ln_g[layer, 0], ln_b[layer, 0])
        mk = (mem_prompt @ xa_w_k[layer]).reshape(bp, -1, X_HEADS, HEAD_DIM)
        mv = (mem_prompt @ xa_w_v[layer]).reshape(bp, -1, X_HEADS, HEAD_DIM)
        p_mem_k.append(mk); p_mem_v.append(mv)
        xp = post_norm(xp, mem_xattn(xp, mk, mv, xa_w_q[layer], xa_w_o[layer]), ln_g[layer, 1], ln_b[layer, 1])
        xs = post_norm(xs, mem_xattn(xs, cache_mem_k[layer], cache_mem_v[layer], xa_w_q[layer], xa_w_o[layer]),
                       ln_g[layer, 1], ln_b[layer, 1])
        fp, fb = conv_ffn(xp, jnp.zeros((bp, FFN_CONV - 1, 2 * D_FF), dt),
                          ffn_w_up[layer], ffn_conv_w[layer], ffn_conv_b[layer], ffn_w_down[layer])
        p_ffn_conv.append(fb)
        fs, fb = conv_ffn(xs, state_ffn_conv[layer],
                          ffn_w_up[layer], ffn_conv_w[layer], ffn_conv_b[layer], ffn_w_down[layer])
        s_ffn_conv.append(fb)
        xp = post_norm(xp, fp, ln_g[layer, 2], ln_b[layer, 2])
        xs = post_norm(xs, fs, ln_g[layer, 2], ln_b[layer, 2])
    return (xp, xs,
            jnp.stack(p_sb_k), jnp.stack(p_sb_v), jnp.stack(p_dn_conv), jnp.stack(p_dn_S),
            jnp.stack(p_lru_conv), jnp.stack(p_lru_h), jnp.stack(p_mem_k), jnp.stack(p_mem_v),
            jnp.stack(p_ffn_conv),
            jnp.stack(s_sb_k), jnp.stack(s_sb_v), jnp.stack(s_dn_conv), jnp.stack(s_dn_S),
            jnp.stack(s_lru_conv), jnp.stack(s_lru_h), jnp.stack(s_ffn_conv))
```

```python
import functools
import math

import jax
import jax.numpy as jnp
from jax import lax
from jax.experimental import pallas as pl
from jax.experimental.pallas import tpu as pltpu

F32 = jnp.float32
BF16 = jnp.bfloat16

HEAD_DIM = 128
DN_HEADS = 8
SB_HEADS = 8
X_HEADS = 4
DN_CONV = 4
LRU_CONV = 4
FFN_CONV = 3
LRU_BLOCKS = 8
LRU_C = 8.0
PAGE_SIZE = 128
DEPTH = 2
DEEPNORM_ALPHA = (2.0 * DEPTH) ** 0.25
LN_EPS = 1e-5
NORM_EPS = 1e-6
ATT_SCALE = HEAD_DIM ** -0.5

LANES = 128
SUBLANES = 8
VMEM_LIMIT_BYTES = 56 * 1024 * 1024
BUF_ROWS = SUBLANES


def _cparams(*sem):
    return pltpu.CompilerParams(dimension_semantics=sem, vmem_limit_bytes=VMEM_LIMIT_BYTES)


def _dot(a, b):
    return jnp.dot(a, b, preferred_element_type=F32)


def _dot_nt(a, b):
    return lax.dot_general(a, b, (((1,), (1,)), ((), ())), preferred_element_type=F32)


def _dot_tn(a, b):
    return lax.dot_general(a, b, (((0,), (0,)), ((), ())), preferred_element_type=F32)


def _split2(x):
    hi = x.astype(BF16)
    lo = (x - hi.astype(F32)).astype(BF16)
    return hi, lo


def _sigmoid(x):
    return 1.0 / (1.0 + jnp.exp(-x))


def _neg_softplus(z):
    return jnp.minimum(-z, 0.0) - jnp.log1p(jnp.exp(-jnp.abs(z)))


def _mm_kernel(x_ref, w_ref, o_ref, xb_ref):
    @pl.when(pl.program_id(1) == 0)
    def _():
        xb_ref[...] = x_ref[...].astype(BF16)

    o_ref[...] = _dot(xb_ref[...], w_ref[...]).astype(o_ref.dtype)


def _mm(x, w, *, tm, tn, out_dtype=F32):
    M, K = x.shape
    N = w.shape[1]
    assert M % tm == 0 and N % tn == 0 and w.shape[0] == K
    return pl.pallas_call(
        _mm_kernel,
        grid=(M // tm, N // tn),
        in_specs=[pl.BlockSpec((tm, K), lambda i, j: (i, 0)),
                  pl.BlockSpec((K, tn), lambda i, j: (0, j))],
        out_specs=pl.BlockSpec((tm, tn), lambda i, j: (i, j)),
        out_shape=jax.ShapeDtypeStruct((M, N), out_dtype),
        scratch_shapes=[pltpu.VMEM((tm, K), BF16)],
        compiler_params=_cparams("parallel", "arbitrary"),
        name="proj",
    )(x, w)


def _mm_ln_kernel(x_ref, w_ref, r_ref, g_ref, b_ref, o_ref, acc_ref, *, nk):
    k = pl.program_id(1)
    part = _dot(x_ref[...].astype(BF16), w_ref[...])

    @pl.when(k == 0)
    def _():
        acc_ref[...] = part

    @pl.when(k > 0)
    def _():
        acc_ref[...] += part

    @pl.when(k == nk - 1)
    def _():
        h = DEEPNORM_ALPHA * r_ref[...] + acc_ref[...]
        hc = h - jnp.mean(h, axis=-1, keepdims=True)
        var = jnp.mean(hc * hc, axis=-1, keepdims=True)
        o_ref[...] = hc * lax.rsqrt(var + LN_EPS) * g_ref[...] + b_ref[...]


def _mm_ln(x, w, resid, g, b, *, tm, tk):
    M, K = x.shape
    N = w.shape[1]
    tm = min(tm, M)
    assert M % tm == 0 and K % tk == 0
    nk = K // tk
    return pl.pallas_call(
        functools.partial(_mm_ln_kernel, nk=nk),
        grid=(M // tm, nk),
        in_specs=[pl.BlockSpec((tm, tk), lambda i, k: (i, k)),
                  pl.BlockSpec((tk, N), lambda i, k: (k, 0)),
                  pl.BlockSpec((tm, N), lambda i, k: (i, 0)),
                  pl.BlockSpec((1, N), lambda i, k: (0, 0)),
                  pl.BlockSpec((1, N), lambda i, k: (0, 0))],
        out_specs=pl.BlockSpec((tm, N), lambda i, k: (i, 0)),
        out_shape=jax.ShapeDtypeStruct((M, N), F32),
        scratch_shapes=[pltpu.VMEM((tm, N), F32)],
        compiler_params=_cparams("parallel", "arbitrary"),
        name="proj_ln",
    )(x, w, resid, g.reshape(1, N), b.reshape(1, N))


def _conv_taps(u_ref, buf_ref, w_ref, ext_ref, width):
    T = u_ref.shape[1]
    ext_ref[:, 0:BUF_ROWS, :] = buf_ref[...]
    ext_ref[:, BUF_ROWS:BUF_ROWS + T, :] = u_ref[...]
    y = None
    for j in range(width):
        off = BUF_ROWS - (width - 1) + j
        term = ext_ref[:, off:off + T, :] * w_ref[j:j + 1, :][None]
        y = term if y is None else y + term
    return y


def _conv_dn_kernel(u_ref, buf_ref, w_ref, o_ref, ext_ref):
    c = pl.program_id(1)
    y = _conv_taps(u_ref, buf_ref, w_ref, ext_ref, DN_CONV)
    y = y * _sigmoid(y)
    ss = jnp.sum(y * y, axis=-1, keepdims=True)
    scale = jnp.where(c < DN_HEADS, ATT_SCALE, 1.0).astype(F32)
    yn = y * lax.rsqrt(ss + NORM_EPS) * scale
    o_ref[...] = jnp.where(c < 2 * DN_HEADS, yn, y)


def _conv_dn(u3, bufpad, w, *, G):
    B, T, _ = u3.shape
    C = w.shape[1]
    nc = C // HEAD_DIM
    return pl.pallas_call(
        _conv_dn_kernel,
        grid=(B // G, nc),
        in_specs=[pl.BlockSpec((G, T, HEAD_DIM), lambda b, c: (b, 0, c)),
                  pl.BlockSpec((G, BUF_ROWS, HEAD_DIM), lambda b, c: (b, 0, c)),
                  pl.BlockSpec((DN_CONV, HEAD_DIM), lambda b, c: (0, c))],
        out_specs=pl.BlockSpec((G, T, HEAD_DIM), lambda b, c: (b, 0, c)),
        out_shape=jax.ShapeDtypeStruct((B, T, C), F32),
        scratch_shapes=[pltpu.VMEM((G, BUF_ROWS + T, HEAD_DIM), F32)],
        compiler_params=_cparams("parallel", "parallel"),
        name="conv_dn",
    )(u3, bufpad, w)


def _conv_lin_kernel(u_ref, buf_ref, w_ref, b_ref, o_ref, ext_ref):
    y = _conv_taps(u_ref, buf_ref, w_ref, ext_ref, LRU_CONV)
    o_ref[...] = y + b_ref[...][None]


def _conv_lin(u3, col0, bufpad, w, bias, *, G, tc):
    B, T, _ = u3.shape
    C = w.shape[1]
    return pl.pallas_call(
        _conv_lin_kernel,
        grid=(B // G, C // tc),
        in_specs=[pl.BlockSpec((G, T, tc), lambda b, c: (b, 0, col0 + c)),
                  pl.BlockSpec((G, BUF_ROWS, tc), lambda b, c: (b, 0, c)),
                  pl.BlockSpec((LRU_CONV, tc), lambda b, c: (0, c)),
                  pl.BlockSpec((1, tc), lambda b, c: (0, c))],
        out_specs=pl.BlockSpec((G, T, tc), lambda b, c: (b, 0, c)),
        out_shape=jax.ShapeDtypeStruct((B, T, C), F32),
        scratch_shapes=[pltpu.VMEM((G, BUF_ROWS + T, tc), F32)],
        compiler_params=_cparams("parallel", "parallel"),
        name="conv_lin",
    )(u3, bufpad, w, bias.reshape(1, C))


def _conv_ffn_kernel(ug_ref, uv_ref, bg_ref, bv_ref, wg_ref, wv_ref, cg_ref, cv_ref, o_ref, ext_ref):
    yg = _conv_taps(ug_ref, bg_ref, wg_ref, ext_ref, FFN_CONV) + cg_ref[...][None]
    yv = _conv_taps(uv_ref, bv_ref, wv_ref, ext_ref, FFN_CONV) + cv_ref[...][None]
    o_ref[...] = (yg * _sigmoid(yg) * yv).astype(o_ref.dtype)


def _conv_ffn(u3, bufpad, w, bias, *, G, tc):
    B, T, C2 = u3.shape
    C = C2 // 2
    nc = C // tc
    return pl.pallas_call(
        _conv_ffn_kernel,
        grid=(B // G, nc),
        in_specs=[pl.BlockSpec((G, T, tc), lambda b, c: (b, 0, c)),
                  pl.BlockSpec((G, T, tc), lambda b, c: (b, 0, nc + c)),
                  pl.BlockSpec((G, BUF_ROWS, tc), lambda b, c: (b, 0, c)),
                  pl.BlockSpec((G, BUF_ROWS, tc), lambda b, c: (b, 0, nc + c)),
                  pl.BlockSpec((FFN_CONV, tc), lambda b, c: (0, c)),
                  pl.BlockSpec((FFN_CONV, tc), lambda b, c: (0, nc + c)),
                  pl.BlockSpec((1, tc), lambda b, c: (0, c)),
                  pl.BlockSpec((1, tc), lambda b, c: (0, nc + c))],
        out_specs=pl.BlockSpec((G, T, tc), lambda b, c: (b, 0, c)),
        out_shape=jax.ShapeDtypeStruct((B, T, C), BF16),
        scratch_shapes=[pltpu.VMEM((G, BUF_ROWS + T, tc), F32)],
        compiler_params=_cparams("parallel", "parallel"),
        name="conv_ffn",
    )(u3, u3, bufpad, bufpad, w, w, bias.reshape(1, C2), bias.reshape(1, C2))


def _lane_col(x, lane_iota, idx):
    return jnp.sum(jnp.where(lane_iota == idx, x, 0.0), axis=1, keepdims=True)


def _delta_kernel(q_ref, k_ref, v_ref, gate_ref, ab_ref, s0_ref, alog_ref, dtb_ref, nw_ref,
                  o_ref, sout_ref, s_scr, *, C, nchunks, wide):
    n = pl.program_id(1)

    @pl.when(n == 0)
    def _():
        s_scr[...] = s0_ref[0]

    def cast(x):
        return x.astype(BF16) if wide else x

    def dot3(a, b):
        if not wide:
            return jnp.dot(a, b, preferred_element_type=F32, precision=lax.Precision.HIGHEST)
        ah, al = _split2(a)
        bh, bl = _split2(b)
        return _dot(ah, bh) + _dot(ah, bl) + _dot(al, bh)

    row = lax.broadcasted_iota(jnp.int32, (C, C), 0)
    col = lax.broadcasted_iota(jnp.int32, (C, C), 1)
    eye = row == col
    incl = row >= col
    strict = row > col
    lane = lax.broadcasted_iota(jnp.int32, (C, LANES), 1)

    ab = ab_ref[...]
    z = ab + dtb_ref[...]
    softplus = jnp.maximum(z, 0.0) + jnp.log1p(jnp.exp(-jnp.abs(z)))
    g_all = -jnp.exp(alog_ref[...]) * softplus
    beta_all = _sigmoid(ab)
    tril = jnp.where(incl, 1.0, 0.0).astype(F32)
    gc_all = dot3(tril, g_all)

    for h in range(DN_HEADS):
        sl = slice(h * HEAD_DIM, (h + 1) * HEAD_DIM)
        q = q_ref[:, sl]
        k = k_ref[:, sl]
        v = v_ref[:, sl]
        gcol = _lane_col(gc_all, lane, h)
        bcol = _lane_col(beta_all, lane, DN_HEADS + h)
        grow = jnp.sum(jnp.where(eye, gcol, 0.0), axis=0, keepdims=True)
        decay = jnp.where(incl, jnp.exp(jnp.where(incl, gcol - grow, 0.0)), 0.0)
        kb = k * bcol
        m = jnp.where(strict, _dot_nt(cast(kb), cast(k)) * decay, 0.0)
        npow = -m
        tinv = jnp.where(eye, 1.0, 0.0).astype(F32) + npow
        span = 2
        while span < C:
            npow = dot3(npow, npow)
            tinv = tinv + dot3(tinv, npow)
            span *= 2
        eg = jnp.exp(gcol)
        tb = cast(tinv)
        u = _dot(tb, cast(v * bcol))
        w = _dot(tb, cast(kb * eg))
        attn = jnp.where(incl, _dot_nt(cast(q), cast(k)) * decay, 0.0)
        glast = gcol[C - 1:C, :]
        qg = q * eg
        kg = k * jnp.exp(glast - gcol)
        s = s_scr[h]
        sb = cast(s)
        v_new = u - _dot(cast(w), sb)
        o = _dot(cast(qg), sb) + _dot(cast(attn), cast(v_new))
        s_scr[h] = s * jnp.exp(glast) + _dot_tn(cast(kg), cast(v_new))
        o = o * lax.rsqrt(jnp.mean(o * o, axis=-1, keepdims=True) + NORM_EPS) * nw_ref[...]
        gate = gate_ref[:, sl]
        o_ref[:, sl] = (o * (gate * _sigmoid(gate))).astype(o_ref.dtype)

    @pl.when(n == nchunks - 1)
    def _():
        sout_ref[0] = s_scr[...]


def _delta(qkvn, proj, ab, s0, a_log, dt_bias, norm_w, *, B, T, C):
    nchunks = T // C
    W = DN_HEADS * HEAD_DIM
    pad = LANES - DN_HEADS
    alog_p = jnp.pad(a_log.astype(F32), (0, pad)).reshape(1, LANES)
    dtb_p = jnp.pad(dt_bias.astype(F32), (0, pad)).reshape(1, LANES)
    rows = lambda b, n: b * nchunks + n
    return pl.pallas_call(
        functools.partial(_delta_kernel, C=C, nchunks=nchunks, wide=C >= 16),
        grid=(B, nchunks),
        in_specs=[pl.BlockSpec((C, W), lambda b, n: (rows(b, n), 0)),
                  pl.BlockSpec((C, W), lambda b, n: (rows(b, n), 1)),
                  pl.BlockSpec((C, W), lambda b, n: (rows(b, n), 2)),
                  pl.BlockSpec((C, W), lambda b, n: (rows(b, n), 3)),
                  pl.BlockSpec((C, LANES), lambda b, n: (rows(b, n), 0)),
                  pl.BlockSpec((1, DN_HEADS, HEAD_DIM, HEAD_DIM), lambda b, n: (b, 0, 0, 0)),
                  pl.BlockSpec((1, LANES), lambda b, n: (0, 0)),
                  pl.BlockSpec((1, LANES), lambda b, n: (0, 0)),
                  pl.BlockSpec((1, HEAD_DIM), lambda b, n: (0, 0))],
        out_specs=[pl.BlockSpec((C, W), lambda b, n: (rows(b, n), 0)),
                   pl.BlockSpec((1, DN_HEADS, HEAD_DIM, HEAD_DIM), lambda b, n: (b, 0, 0, 0))],
        out_shape=[jax.ShapeDtypeStruct((B * T, W), BF16),
                   jax.ShapeDtypeStruct((B, DN_HEADS, HEAD_DIM, HEAD_DIM), F32)],
        scratch_shapes=[pltpu.VMEM((DN_HEADS, HEAD_DIM, HEAD_DIM), F32)],
        compiler_params=_cparams("parallel", "arbitrary"),
        name="delta_rule",
    )(qkvn, qkvn, qkvn, proj, ab, s0, alog_p, dtb_p, norm_w.reshape(1, HEAD_DIM))


def _heads_to_lanes(ref, nheads, ntok):
    return jnp.concatenate(
        [ref[pl.ds(h, ntok, stride=nheads), :].astype(BF16) for h in range(nheads)], axis=1)


def _sb_block(qb, kblk, vblk, bias, carry, mask, ustrict):
    z = _dot_nt(qb, kblk) * ATT_SCALE + bias
    lk = _neg_softplus(z)
    ls = z + lk
    if mask is not None:
        lk = jnp.where(mask, lk, 0.0)
    hi, lo = _split2(lk)
    r = _dot(hi, ustrict) + _dot(lo, ustrict)
    a = jnp.exp(ls + r + carry)
    if mask is not None:
        a = jnp.where(mask, a, 0.0)
    return _dot(a.astype(BF16), vblk), jnp.sum(lk, axis=1, keepdims=True)


def _sbp_kernel(bias_ref, q_ref, k_ref, v_ref, o_ref, kb_scr, vb_scr, *, tq):
    h = pl.program_id(1)
    qi = pl.program_id(2)

    @pl.when(qi == 0)
    def _():
        kb_scr[...] = k_ref[...].astype(BF16)
        vb_scr[...] = v_ref[...].astype(BF16)

    qb = q_ref[...].astype(BF16)
    bias = bias_ref[h]
    row = lax.broadcasted_iota(jnp.int32, (tq, tq), 0)
    col = lax.broadcasted_iota(jnp.int32, (tq, tq), 1)
    ustrict = jnp.where(row > col, 1.0, 0.0).astype(BF16)
    causal = col < row

    def blk(kj, carry, mask):
        ks = pl.multiple_of(kj * tq, tq)
        return _sb_block(qb, kb_scr[pl.ds(ks, tq), :], vb_scr[pl.ds(ks, tq), :], bias, carry, mask, ustrict)

    acc, carry = blk(qi, jnp.zeros((tq, 1), F32), causal)

    def body(it, st):
        acc, carry = st
        pv, rs = blk(qi - 1 - it, carry, None)
        return acc + pv, carry + rs

    acc, _ = lax.fori_loop(0, qi, body, (acc, carry))
    o_ref[...] = acc.astype(o_ref.dtype)


def _sb_prompt(proj, sb_bias, *, B, T, qcol, kcol, vcol, tq):
    nq = T // tq
    grid_spec = pltpu.PrefetchScalarGridSpec(
        num_scalar_prefetch=1,
        grid=(B, SB_HEADS, nq),
        in_specs=[pl.BlockSpec((tq, HEAD_DIM), lambda b, h, i, s: (b * nq + i, qcol + h)),
                  pl.BlockSpec((T, HEAD_DIM), lambda b, h, i, s: (b, kcol + h)),
                  pl.BlockSpec((T, HEAD_DIM), lambda b, h, i, s: (b, vcol + h))],
        out_specs=pl.BlockSpec((tq, HEAD_DIM), lambda b, h, i, s: (b * nq + i, h)),
        scratch_shapes=[pltpu.VMEM((T, HEAD_DIM), BF16), pltpu.VMEM((T, HEAD_DIM), BF16)],
    )
    return pl.pallas_call(
        functools.partial(_sbp_kernel, tq=tq),
        grid_spec=grid_spec,
        out_shape=jax.ShapeDtypeStruct((B * T, SB_HEADS * HEAD_DIM), BF16),
        compiler_params=_cparams("parallel", "parallel", "arbitrary"),
        name="sb_prompt",
    )(sb_bias.astype(F32), proj, proj, proj)


def _sbs_kernel(pt_ref, q_ref, kn_ref, vn_ref, kc_ref, vc_ref, bias_ref, o_ref,
                qbd_scr, knp_scr, vnp_scr, acc_scr, carry_scr, *, T, npages):
    p = pl.program_id(1)
    R = SB_HEADS * T
    W = SB_HEADS * HEAD_DIM
    rrow = lax.broadcasted_iota(jnp.int32, (R, W), 0)
    rcol = lax.broadcasted_iota(jnp.int32, (R, W), 1)
    own_head = (rrow // T) == (rcol // HEAD_DIM)
    row = lax.broadcasted_iota(jnp.int32, (PAGE_SIZE, PAGE_SIZE), 0)
    col = lax.broadcasted_iota(jnp.int32, (PAGE_SIZE, PAGE_SIZE), 1)
    ustrict = jnp.where(row > col, 1.0, 0.0).astype(BF16)
    bias = bias_ref[...]

    def step(kblk, vblk, mask):
        pv, rs = _sb_block(qbd_scr[...], kblk, vblk, bias, carry_scr[...], mask, ustrict)
        acc_scr[...] += pv
        carry_scr[...] += rs

    @pl.when(p == 0)
    def _():
        q = q_ref[...]
        qrep = jnp.concatenate([q] * SB_HEADS, axis=0)
        qbd_scr[...] = jnp.where(own_head, qrep, 0.0).astype(BF16)
        knp_scr[...] = jnp.zeros_like(knp_scr)
        vnp_scr[...] = jnp.zeros_like(vnp_scr)
        knp_scr[0:T, :] = kn_ref[...]
        vnp_scr[0:T, :] = vn_ref[...]
        acc_scr[...] = jnp.zeros_like(acc_scr)
        carry_scr[...] = jnp.zeros_like(carry_scr)
        qpos = lax.broadcasted_iota(jnp.int32, (R, PAGE_SIZE), 0) % T
        kpos = lax.broadcasted_iota(jnp.int32, (R, PAGE_SIZE), 1)
        step(knp_scr[...].astype(BF16), vnp_scr[...].astype(BF16), kpos < qpos)

    step(_heads_to_lanes(kc_ref.at[0], SB_HEADS, PAGE_SIZE),
         _heads_to_lanes(vc_ref.at[0], SB_HEADS, PAGE_SIZE), None)

    @pl.when(p == npages - 1)
    def _():
        accm = jnp.where(own_head, acc_scr[...], 0.0)
        out = accm[0:T]
        for h in range(1, SB_HEADS):
            out = out + accm[h * T:(h + 1) * T]
        o_ref[...] = out.astype(o_ref.dtype)


def _sb_sample(proj, cache_k, cache_v, page_table, sb_bias, *, B, T, qcol, kcol, vcol):
    npages = page_table.shape[1]
    W = SB_HEADS * HEAD_DIM
    R = SB_HEADS * T
    bias_rows = jnp.broadcast_to(jnp.repeat(sb_bias.astype(F32), T)[:, None], (R, PAGE_SIZE))
    grid_spec = pltpu.PrefetchScalarGridSpec(
        num_scalar_prefetch=1,
        grid=(B, npages),
        in_specs=[pl.BlockSpec((T, W), lambda b, p, pt: (b, qcol)),
                  pl.BlockSpec((T, W), lambda b, p, pt: (b, kcol)),
                  pl.BlockSpec((T, W), lambda b, p, pt: (b, vcol)),
                  pl.BlockSpec((1, PAGE_SIZE * SB_HEADS, HEAD_DIM), lambda b, p, pt: (pt[b, npages - 1 - p], 0, 0)),
                  pl.BlockSpec((1, PAGE_SIZE * SB_HEADS, HEAD_DIM), lambda b, p, pt: (pt[b, npages - 1 - p], 0, 0)),
                  pl.BlockSpec((R, PAGE_SIZE), lambda b, p, pt: (0, 0))],
        out_specs=pl.BlockSpec((T, W), lambda b, p, pt: (b, 0)),
        scratch_shapes=[pltpu.VMEM((R, W), BF16),
                        pltpu.VMEM((PAGE_SIZE, W), F32),
                        pltpu.VMEM((PAGE_SIZE, W), F32),
                        pltpu.VMEM((R, W), F32),
                        pltpu.VMEM((R, 1), F32)],
    )
    return pl.pallas_call(
        functools.partial(_sbs_kernel, T=T, npages=npages),
        grid_spec=grid_spec,
        out_shape=jax.ShapeDtypeStruct((B * T, W), BF16),
        compiler_params=_cparams("arbitrary", "arbitrary"),
        name="sb_sample",
    )(page_table, proj, proj, proj, cache_k, cache_v, bias_rows)


def _xattn_prompt_kernel(q_ref, k_ref, v_ref, o_ref):
    for h in range(X_HEADS):
        sl = slice(h * HEAD_DIM, (h + 1) * HEAD_DIM)
        s = _dot_nt(q_ref[:, sl].astype(BF16), k_ref[0, :, sl].astype(BF16)) * ATT_SCALE
        p = jnp.exp(s - jnp.max(s, axis=-1, keepdims=True))
        p = p / jnp.sum(p, axis=-1, keepdims=True)
        o_ref[:, sl] = _dot(p.astype(BF16), v_ref[0, :, sl].astype(BF16)).astype(o_ref.dtype)


def _xattn_prompt(q, mk, mv, *, B, T, tq):
    W = X_HEADS * HEAD_DIM
    nq = T // tq
    nm = mk.shape[1]
    return pl.pallas_call(
        _xattn_prompt_kernel,
        grid=(B, nq),
        in_specs=[pl.BlockSpec((tq, W), lambda b, i: (b * nq + i, 0)),
                  pl.BlockSpec((1, nm, W), lambda b, i: (b, 0, 0)),
                  pl.BlockSpec((1, nm, W), lambda b, i: (b, 0, 0))],
        out_specs=pl.BlockSpec((tq, W), lambda b, i: (b * nq + i, 0)),
        out_shape=jax.ShapeDtypeStruct((B * T, W), BF16),
        compiler_params=_cparams("parallel", "parallel"),
        name="xattn_prompt",
    )(q, mk, mv)


def _xattn_sample_kernel(q_ref, k_ref, v_ref, o_ref, *, G, T, nm):
    R = X_HEADS * T
    W = X_HEADS * HEAD_DIM
    rrow = lax.broadcasted_iota(jnp.int32, (R, W), 0)
    rcol = lax.broadcasted_iota(jnp.int32, (R, W), 1)
    own_head = (rrow // T) == (rcol // HEAD_DIM)
    for g in range(G):
        q = q_ref[g]
        qbd = jnp.where(own_head, jnp.concatenate([q] * X_HEADS, axis=0), 0.0).astype(BF16)
        s = _dot_nt(qbd, _heads_to_lanes(k_ref.at[g], X_HEADS, nm)) * ATT_SCALE
        p = jnp.exp(s - jnp.max(s, axis=-1, keepdims=True))
        p = p / jnp.sum(p, axis=-1, keepdims=True)
        of = jnp.where(own_head, _dot(p.astype(BF16), _heads_to_lanes(v_ref.at[g], X_HEADS, nm)), 0.0)
        out = of[0:T]
        for h in range(1, X_HEADS):
            out = out + of[h * T:(h + 1) * T]
        o_ref[g] = out.astype(o_ref.dtype)


def _xattn_sample(q3, mk, mv, *, G):
    B, T, W = q3.shape
    nm = mk.shape[1] // X_HEADS
    return pl.pallas_call(
        functools.partial(_xattn_sample_kernel, G=G, T=T, nm=nm),
        grid=(B // G,),
        in_specs=[pl.BlockSpec((G, T, W), lambda b: (b, 0, 0)),
                  pl.BlockSpec((G, nm * X_HEADS, HEAD_DIM), lambda b: (b, 0, 0)),
                  pl.BlockSpec((G, nm * X_HEADS, HEAD_DIM), lambda b: (b, 0, 0))],
        out_specs=pl.BlockSpec((G, T, W), lambda b: (b, 0, 0)),
        out_shape=jax.ShapeDtypeStruct((B, T, W), BF16),
        compiler_params=_cparams("parallel"),
        name="xattn_sample",
    )(q3, mk, mv)


def _lru_kernel(y_ref, xc_ref, wr_ref, wi_ref, br_ref, bi_ref, lam_ref, h0_ref, o_ref, hl_ref,
                hc_scr, a_scr, b_scr, h_scr, *, G, Tt, nt):
    t = pl.program_id(2)
    rows = G * Tt
    ngroups = rows // SUBLANES
    tc = a_scr.shape[1]

    @pl.when(t == 0)
    def _():
        hc_scr[...] = h0_ref[...]

    xc = xc_ref[...].reshape(rows, tc)
    xb = xc.astype(BF16)
    r = _sigmoid(_dot(xb, wr_ref[0]) + br_ref[...])
    i = _sigmoid(_dot(xb, wi_ref[0]) + bi_ref[...])
    log_a = LRU_C * r * _neg_softplus(-lam_ref[...])
    a = jnp.exp(log_a)
    a_scr[...] = a
    b_scr[...] = jnp.sqrt(-jnp.tanh(log_a) * (a * a + 1.0)) * (i * xc)
    sub = lax.broadcasted_iota(jnp.int32, (SUBLANES, tc), 0)

    def group(gi, carry):
        r0 = pl.multiple_of(gi * SUBLANES, SUBLANES)
        a = a_scr[pl.ds(r0, SUBLANES), :]
        b = b_scr[pl.ds(r0, SUBLANES), :]
        for s in (1, 2, 4):
            keep = sub >= s
            a_sh = jnp.where(keep, pltpu.roll(a, s, 0), 1.0)
            b_sh = jnp.where(keep, pltpu.roll(b, s, 0), 0.0)
            b = a * b_sh + b
            a = a * a_sh
        hprev = hc_scr[gi] if Tt == SUBLANES else carry
        h8 = b + a * hprev
        h_scr[pl.ds(r0, SUBLANES), :] = h8
        last = h8[SUBLANES - 1:SUBLANES, :]
        if Tt == SUBLANES:
            hl_ref[gi] = last
            return carry
        return last

    carry = lax.fori_loop(0, ngroups, group, hc_scr[0])
    if Tt != SUBLANES:
        hc_scr[0] = carry

        @pl.when(t == nt - 1)
        def _():
            hl_ref[0] = carry

    y = y_ref[...].reshape(rows, tc)
    o_ref[...] = (jax.nn.gelu(y) * h_scr[...]).astype(o_ref.dtype).reshape(G, Tt, tc)


def _lru(proj3, xc3, w_r, w_i, b_r, b_i, lam, h0, *, G, Tt):
    B, T, D = xc3.shape
    tc = D // LRU_BLOCKS
    nt = T // Tt
    assert Tt % SUBLANES == 0 and (G == 1 or Tt == SUBLANES)
    vec = lambda a: a.astype(F32).reshape(1, D)
    return pl.pallas_call(
        functools.partial(_lru_kernel, G=G, Tt=Tt, nt=nt),
        grid=(B // G, LRU_BLOCKS, nt),
        in_specs=[pl.BlockSpec((G, Tt, tc), lambda b, c, t: (b, t, c)),
                  pl.BlockSpec((G, Tt, tc), lambda b, c, t: (b, t, c)),
                  pl.BlockSpec((1, tc, tc), lambda b, c, t: (c, 0, 0)),
                  pl.BlockSpec((1, tc, tc), lambda b, c, t: (c, 0, 0)),
                  pl.BlockSpec((1, tc), lambda b, c, t: (0, c)),
                  pl.BlockSpec((1, tc), lambda b, c, t: (0, c)),
                  pl.BlockSpec((1, tc), lambda b, c, t: (0, c)),
                  pl.BlockSpec((G, 1, tc), lambda b, c, t: (b, 0, c))],
        out_specs=[pl.BlockSpec((G, Tt, tc), lambda b, c, t: (b, t, c)),
                   pl.BlockSpec((G, 1, tc), lambda b, c, t: (b, 0, c))],
        out_shape=[jax.ShapeDtypeStruct((B, T, D), BF16),
                   jax.ShapeDtypeStruct((B, 1, D), F32)],
        scratch_shapes=[pltpu.VMEM((G, 1, tc), F32),
                        pltpu.VMEM((G * Tt, tc), F32),
                        pltpu.VMEM((G * Tt, tc), F32),
                        pltpu.VMEM((G * Tt, tc), F32)],
        compiler_params=_cparams("parallel", "parallel", "arbitrary"),
        name="rglru",
    )(proj3, xc3, w_r, w_i, vec(b_r), vec(b_i), vec(lam), h0.reshape(B, 1, D))


def _pad_buf(buf):
    return jnp.pad(buf, ((0, 0), (BUF_ROWS - buf.shape[1], 0), (0, 0)))


def _row_tile(m):
    return 512 if m % 512 == 0 else m


class _Stream:
    def __init__(self, B, T):
        self.B, self.T = B, T
        self.M = B * T
        self.long = T > SUBLANES
        self.G = 1 if self.long else B


def _ab_layer(st, x, w, dn_conv, dn_S, sb_fn):
    B, T, M = st.B, st.T, st.M
    tm = _row_tile(M)
    proj = _mm(x, w["ab_main"], tm=tm, tn=512)
    ab = _mm(x, w["ab_ab"], tm=tm, tn=LANES)
    proj3 = proj.reshape(B, T, -1)
    nqkv = 3 * DN_HEADS * HEAD_DIM
    qkvn = _conv_dn(proj3, _pad_buf(dn_conv), w["ab_conv_w"], G=st.G)
    C = 64 if st.long else T
    o_dn, s_new = _delta(qkvn.reshape(M, nqkv), proj, ab, dn_S, w["dn_A_log"], w["dn_dt_bias"],
                         w["dn_norm_w"], B=B, T=T, C=C)
    o_sb = sb_fn(proj)
    o = jnp.concatenate([o_dn, o_sb], axis=-1)
    new_conv = proj3[:, T - (DN_CONV - 1):, :nqkv]
    W = SB_HEADS * HEAD_DIM
    k_new = proj[:, 5 * W:6 * W].reshape(B, T, SB_HEADS, HEAD_DIM)
    v_new = proj[:, 6 * W:7 * W].reshape(B, T, SB_HEADS, HEAD_DIM)
    return o, w["ab_w_out"], (k_new, v_new, new_conv, s_new)


def _lru_layer(st, x, w, lru_conv, lru_h):
    B, T, M = st.B, st.T, st.M
    D = x.shape[1]
    proj = _mm(x, w["c_w_in"], tm=_row_tile(M), tn=512)
    proj3 = proj.reshape(B, T, 2 * D)
    tc = 256
    xc3 = _conv_lin(proj3, D // tc, _pad_buf(lru_conv), w["lru_conv_w"], w["lru_conv_b"], G=st.G, tc=tc)
    Tt = 512 if st.long else T
    o3, h_last = _lru(proj3, xc3, w["lru_w_r"], w["lru_w_i"], w["lru_b_r"], w["lru_b_i"],
                      w["lru_lambda"], lru_h, G=st.G, Tt=Tt)
    new_conv = proj3[:, T - (LRU_CONV - 1):, D:]
    return o3.reshape(M, D), w["c_w_out"], (new_conv, h_last.reshape(B, D))


def _xattn_layer(st, x, w, mk3, mv3):
    B, T, M = st.B, st.T, st.M
    q = _mm(x, w["xa_w_q"], tm=_row_tile(M), tn=512)
    if st.long:
        return _xattn_prompt(q, mk3, mv3, B=B, T=T, tq=512)
    W = q.shape[1]
    return _xattn_sample(q.reshape(B, T, W), mk3, mv3, G=8).reshape(M, W)


def _ffn_layer(st, x, w, ffn_conv, d_ff):
    B, T, M = st.B, st.T, st.M
    ffp = w["ffn_conv_w"].shape[1] // 2
    u = _mm(x, w["ffn_w_up"], tm=_row_tile(M), tn=512)
    u3 = u.reshape(B, T, 2 * ffp)
    buf = ffn_conv
    bufp = jnp.concatenate([jnp.pad(buf[..., :d_ff], ((0, 0), (0, 0), (0, ffp - d_ff))),
                            jnp.pad(buf[..., d_ff:], ((0, 0), (0, 0), (0, ffp - d_ff)))], axis=-1)
    h3 = _conv_ffn(u3, _pad_buf(bufp), w["ffn_conv_w"], w["ffn_conv_b"], G=st.G, tc=256 if st.long else 512)
    tail = u3[:, T - (FFN_CONV - 1):, :]
    new_conv = jnp.concatenate([tail[..., :d_ff], tail[..., ffp:ffp + d_ff]], axis=-1)
    return h3.reshape(M, ffp), new_conv


def kernel(x_prompt, x_sample, mem_prompt, cache_sb_k, cache_sb_v, cache_mem_k, cache_mem_v, state_dn_conv, state_dn_S, state_lru_conv, state_lru_h, state_ffn_conv, page_table, ab_w_in, ab_conv_w, dn_A_log, dn_dt_bias, dn_norm_w, sb_bias, ab_w_out, c_w_in, lru_conv_w, lru_conv_b, lru_w_r, lru_b_r, lru_w_i, lru_b_i, lru_lambda, c_w_out, xa_w_q, xa_w_k, xa_w_v, xa_w_o, ffn_w_up, ffn_conv_w, ffn_conv_b, ffn_w_down, ln_g, ln_b):
    bp, tp, d_model = x_prompt.shape
    bs, ts, _ = x_sample.shape
    sp, ss = _Stream(bp, tp), _Stream(bs, ts)
    dt = x_prompt.dtype
    n_mem = mem_prompt.shape[1]
    xw = X_HEADS * HEAD_DIM
    sbw = SB_HEADS * HEAD_DIM
    nqkv = 3 * DN_HEADS * HEAD_DIM
    d_ff = ffn_w_down.shape[1]
    ffp = -(-d_ff // 512) * 512
    depth = ln_g.shape[0]

    xp = x_prompt.reshape(sp.M, d_model)
    xs = x_sample.reshape(ss.M, d_model)
    mem2 = mem_prompt.reshape(bp * n_mem, d_model)

    outs_p = {k: [] for k in ("sb_k", "sb_v", "dn_conv", "dn_S", "lru_conv", "lru_h", "mem_k", "mem_v", "ffn_conv")}
    outs_s = {k: [] for k in ("sb_k", "sb_v", "dn_conv", "dn_S", "lru_conv", "lru_h", "ffn_conv")}

    def pad_ff(a, axis):
        pad = [(0, 0)] * a.ndim
        pad[axis] = (0, ffp - d_ff)
        return jnp.pad(a, pad)

    for layer in range(depth):
        w = {}
        if layer % 2 == 0:
            e = layer // 2
            w_in = ab_w_in[e]
            c_ab = nqkv + DN_HEADS * HEAD_DIM
            w["ab_main"] = jnp.concatenate([w_in[:, :c_ab], w_in[:, c_ab + 2 * DN_HEADS:]], axis=1).astype(BF16)
            w["ab_ab"] = jnp.pad(w_in[:, c_ab:c_ab + 2 * DN_HEADS], ((0, 0), (0, LANES - 2 * DN_HEADS))).astype(BF16)
            w["ab_conv_w"] = ab_conv_w[e]
            w["dn_A_log"], w["dn_dt_bias"], w["dn_norm_w"] = dn_A_log[e], dn_dt_bias[e], dn_norm_w[e]
            w["ab_w_out"] = ab_w_out[e].astype(BF16)
        else:
            o = layer // 2
            w["c_w_in"] = c_w_in[o].astype(BF16)
            w["lru_conv_w"], w["lru_conv_b"] = lru_conv_w[o], lru_conv_b[o]
            w["lru_w_r"], w["lru_w_i"] = lru_w_r[o].astype(BF16), lru_w_i[o].astype(BF16)
            w["lru_b_r"], w["lru_b_i"], w["lru_lambda"] = lru_b_r[o], lru_b_i[o], lru_lambda[o]
            w["c_w_out"] = c_w_out[o].astype(BF16)
        w["xa_w_q"] = xa_w_q[layer].astype(BF16)
        w["ffn_w_up"] = jnp.concatenate([pad_ff(ffn_w_up[layer][:, :d_ff], 1),
                                         pad_ff(ffn_w_up[layer][:, d_ff:], 1)], axis=1).astype(BF16)
        w["ffn_conv_w"] = jnp.concatenate([pad_ff(ffn_conv_w[layer][:, :d_ff], 1),
                                           pad_ff(ffn_conv_w[layer][:, d_ff:], 1)], axis=1)
        w["ffn_conv_b"] = jnp.concatenate([pad_ff(ffn_conv_b[layer][:d_ff], 0),
                                           pad_ff(ffn_conv_b[layer][d_ff:], 0)], axis=0)
        w_down = pad_ff(ffn_w_down[layer], 0).astype(BF16)
        w_xo = xa_w_o[layer].astype(BF16)
        tk_down = ffp // 4 if (ffp // 4) % LANES == 0 else ffp

        if layer % 2 == 0:
            e = layer // 2
            qcol, kcol, vcol = 4 * DN_HEADS, 5 * DN_HEADS, 6 * DN_HEADS
            sbp = functools.partial(_sb_prompt, sb_bias=sb_bias[e], B=bp, T=tp, qcol=qcol, kcol=kcol, vcol=vcol, tq=256)
            op, wout, (k_new, v_new, cb, s_new) = _ab_layer(
                sp, xp, w, jnp.zeros((bp, DN_CONV - 1, nqkv), dt),
                jnp.zeros((bp, DN_HEADS, HEAD_DIM, HEAD_DIM), F32), sbp)
            outs_p["sb_k"].append(k_new); outs_p["sb_v"].append(v_new)
            outs_p["dn_conv"].append(cb); outs_p["dn_S"].append(s_new)
            n_pool = cache_sb_k.shape[1]
            ck = cache_sb_k[e].reshape(n_pool, PAGE_SIZE * SB_HEADS, HEAD_DIM)
            cv = cache_sb_v[e].reshape(n_pool, PAGE_SIZE * SB_HEADS, HEAD_DIM)
            sbs = functools.partial(_sb_sample, cache_k=ck, cache_v=cv, page_table=page_table, sb_bias=sb_bias[e],
                                    B=bs, T=ts, qcol=4, kcol=5, vcol=6)
            os_, _, (k_new, v_new, cb, s_new) = _ab_layer(ss, xs, w, state_dn_conv[e], state_dn_S[e], sbs)
            outs_s["sb_k"].append(k_new); outs_s["sb_v"].append(v_new)
            outs_s["dn_conv"].append(cb); outs_s["dn_S"].append(s_new)
        else:
            o = layer // 2
            op, wout, (cb, hl) = _lru_layer(sp, xp, w, jnp.zeros((bp, LRU_CONV - 1, d_model), dt),
                                            jnp.zeros((bp, d_model), dt))
            outs_p["lru_conv"].append(cb); outs_p["lru_h"].append(hl)
            os_, _, (cb, hl) = _lru_layer(ss, xs, w, state_lru_conv[o], state_lru_h[o])
            outs_s["lru_conv"].append(cb); outs_s["lru_h"].append(hl)
        xp = _mm_ln(op, wout, xp, ln_g[layer, 0], ln_b[layer, 0], tm=256, tk=1024)
        xs = _mm_ln(os_, wout, xs, ln_g[layer, 0], ln_b[layer, 0], tm=256, tk=1024)

        mk = _mm(mem2, xa_w_k[layer].astype(BF16), tm=_row_tile(bp * n_mem), tn=xw)
        mv = _mm(mem2, xa_w_v[layer].astype(BF16), tm=_row_tile(bp * n_mem), tn=xw)
        outs_p["mem_k"].append(mk.reshape(bp, n_mem, X_HEADS, HEAD_DIM))
        outs_p["mem_v"].append(mv.reshape(bp, n_mem, X_HEADS, HEAD_DIM))
        ap = _xattn_layer(sp, xp, w, mk.reshape(bp, n_mem, xw), mv.reshape(bp, n_mem, xw))
        as_ = _xattn_layer(ss, xs, w, cache_mem_k[layer].reshape(bs, n_mem * X_HEADS, HEAD_DIM),
                           cache_mem_v[layer].reshape(bs, n_mem * X_HEADS, HEAD_DIM))
        xp = _mm_ln(ap, w_xo, xp, ln_g[layer, 1], ln_b[layer, 1], tm=256, tk=xw)
        xs = _mm_ln(as_, w_xo, xs, ln_g[layer, 1], ln_b[layer, 1], tm=256, tk=xw)

        hp, fb = _ffn_layer(sp, xp, w, jnp.zeros((bp, FFN_CONV - 1, 2 * d_ff), dt), d_ff)
        outs_p["ffn_conv"].append(fb)
        hs, fb = _ffn_layer(ss, xs, w, state_ffn_conv[layer], d_ff)
        outs_s["ffn_conv"].append(fb)
        xp = _mm_ln(hp, w_down, xp, ln_g[layer, 2], ln_b[layer, 2], tm=256, tk=tk_down)
        xs = _mm_ln(hs, w_down, xs, ln_g[layer, 2], ln_b[layer, 2], tm=256, tk=tk_down)

    st = lambda xs_: jnp.stack(xs_)
    return (xp.reshape(bp, tp, d_model), xs.reshape(bs, ts, d_model),
            st(outs_p["sb_k"]), st(outs_p["sb_v"]), st(outs_p["dn_conv"]), st(outs_p["dn_S"]),
            st(outs_p["lru_conv"]), st(outs_p["lru_h"]), st(outs_p["mem_k"]), st(outs_p["mem_v"]),
            st(outs_p["ffn_conv"]),
            st(outs_s["sb_k"]), st(outs_s["sb_v"]), st(outs_s["dn_conv"]), st(outs_s["dn_S"]),
            st(outs_s["lru_conv"]), st(outs_s["lru_h"]), st(outs_s["ffn_conv"]))
```

```python
import functools

import jax
import jax.numpy as jnp
from jax import lax
from jax.experimental import pallas as pl
from jax.experimental.pallas import tpu as pltpu

F32 = jnp.float32
BF16 = jnp.bfloat16

HEAD_DIM = 128
DN_HEADS = 8
SB_HEADS = 8
X_HEADS = 4
DN_CONV = 4
LRU_CONV = 4
FFN_CONV = 3
LRU_BLOCKS = 8
LRU_C = 8.0
PAGE_SIZE = 128
DEPTH = 2
DEEPNORM_ALPHA = (2.0 * DEPTH) ** 0.25
LN_EPS = 1e-5
NORM_EPS = 1e-6
ATT_SCALE = HEAD_DIM ** -0.5

LANES = 128
SUBLANES = 8
VMEM_LIMIT_BYTES = 56 * 1024 * 1024
BUF_ROWS = SUBLANES
SB_PAGES_PER_STEP = 4
FFN_COL_TILE = 512


def _cparams(*sem):
    return pltpu.CompilerParams(dimension_semantics=sem, vmem_limit_bytes=VMEM_LIMIT_BYTES)


def _dot(a, b):
    return jnp.dot(a, b, preferred_element_type=F32)


def _dot_nt(a, b):
    return lax.dot_general(a, b, (((1,), (1,)), ((), ())), preferred_element_type=F32)


def _dot_tn(a, b):
    return lax.dot_general(a, b, (((0,), (0,)), ((), ())), preferred_element_type=F32)


def _split2(x):
    hi = x.astype(BF16)
    lo = (x - hi.astype(F32)).astype(BF16)
    return hi, lo


def _dot3(a, b):
    ah, al = _split2(a)
    bh, bl = _split2(b)
    return _dot(ah, bh) + _dot(ah, bl) + _dot(al, bh)


def _sigmoid(x):
    return 1.0 / (1.0 + jnp.exp(-x))


def _neg_softplus(z):
    return jnp.minimum(-z, 0.0) - jnp.log1p(jnp.exp(-jnp.abs(z)))


def _block_cols(w, tn):
    K, N = w.shape
    return w.reshape(K, N // tn, tn).transpose(1, 0, 2)


def _mm_kernel(x_ref, w_ref, o_ref, xb_ref):
    @pl.when(pl.program_id(1) == 0)
    def _():
        xb_ref[...] = x_ref[...].astype(BF16)

    o_ref[...] = _dot(xb_ref[...], w_ref[0]).astype(o_ref.dtype)


def _mm(x, wb, *, tm, out_dtype=F32):
    M, K = x.shape
    nj, _, tn = wb.shape
    assert M % tm == 0 and wb.shape[1] == K
    return pl.pallas_call(
        _mm_kernel,
        grid=(M // tm, nj),
        in_specs=[pl.BlockSpec((tm, K), lambda i, j: (i, 0)),
                  pl.BlockSpec((1, K, tn), lambda i, j: (j, 0, 0))],
        out_specs=pl.BlockSpec((tm, tn), lambda i, j: (i, j)),
        out_shape=jax.ShapeDtypeStruct((M, nj * tn), out_dtype),
        scratch_shapes=[pltpu.VMEM((tm, K), BF16)],
        compiler_params=_cparams("parallel", "arbitrary"),
        name="proj",
    )(x, wb)


def _mm_ln_kernel(x_ref, w_ref, r_ref, g_ref, b_ref, o_ref, acc_ref, *, nk):
    k = pl.program_id(1)
    part = _dot(x_ref[...].astype(BF16), w_ref[...])

    @pl.when(k == 0)
    def _():
        acc_ref[...] = part

    @pl.when(k > 0)
    def _():
        acc_ref[...] += part

    @pl.when(k == nk - 1)
    def _():
        h = DEEPNORM_ALPHA * r_ref[...] + acc_ref[...]
        hc = h - jnp.mean(h, axis=-1, keepdims=True)
        var = jnp.mean(hc * hc, axis=-1, keepdims=True)
        o_ref[...] = hc * lax.rsqrt(var + LN_EPS) * g_ref[...] + b_ref[...]


def _mm_ln(x, w, resid, g, b, *, tm, tk):
    M, K = x.shape
    N = w.shape[1]
    tm = min(tm, M)
    assert M % tm == 0 and K % tk == 0
    nk = K // tk
    return pl.pallas_call(
        functools.partial(_mm_ln_kernel, nk=nk),
        grid=(M // tm, nk),
        in_specs=[pl.BlockSpec((tm, tk), lambda i, k: (i, k)),
                  pl.BlockSpec((tk, N), lambda i, k: (k, 0)),
                  pl.BlockSpec((tm, N), lambda i, k: (i, 0)),
                  pl.BlockSpec((1, N), lambda i, k: (0, 0)),
                  pl.BlockSpec((1, N), lambda i, k: (0, 0))],
        out_specs=pl.BlockSpec((tm, N), lambda i, k: (i, 0)),
        out_shape=jax.ShapeDtypeStruct((M, N), F32),
        scratch_shapes=[pltpu.VMEM((tm, N), F32)],
        compiler_params=_cparams("parallel", "arbitrary"),
        name="proj_ln",
    )(x, w, resid, g.reshape(1, N), b.reshape(1, N))


def _conv_taps(u_ref, buf_ref, w_ref, ext_ref, width):
    T = u_ref.shape[1]
    ext_ref[:, 0:BUF_ROWS, :] = buf_ref[...]
    ext_ref[:, BUF_ROWS:BUF_ROWS + T, :] = u_ref[...]
    y = None
    for j in range(width):
        off = BUF_ROWS - (width - 1) + j
        term = ext_ref[:, off:off + T, :] * w_ref[j:j + 1, :][None]
        y = term if y is None else y + term
    return y


def _conv_dn_kernel(u_ref, buf_ref, w_ref, o_ref, ext_ref):
    c = pl.program_id(1)
    y = _conv_taps(u_ref, buf_ref, w_ref, ext_ref, DN_CONV)
    y = y * _sigmoid(y)
    ss = jnp.sum(y * y, axis=-1, keepdims=True)
    scale = jnp.where(c < DN_HEADS, ATT_SCALE, 1.0).astype(F32)
    yn = y * lax.rsqrt(ss + NORM_EPS) * scale
    o_ref[...] = jnp.where(c < 2 * DN_HEADS, yn, y)


def _conv_dn(u3, bufpad, w, *, G):
    B, T, _ = u3.shape
    C = w.shape[1]
    nc = C // HEAD_DIM
    return pl.pallas_call(
        _conv_dn_kernel,
        grid=(B // G, nc),
        in_specs=[pl.BlockSpec((G, T, HEAD_DIM), lambda b, c: (b, 0, c)),
                  pl.BlockSpec((G, BUF_ROWS, HEAD_DIM), lambda b, c: (b, 0, c)),
                  pl.BlockSpec((DN_CONV, HEAD_DIM), lambda b, c: (0, c))],
        out_specs=pl.BlockSpec((G, T, HEAD_DIM), lambda b, c: (b, 0, c)),
        out_shape=jax.ShapeDtypeStruct((B, T, C), F32),
        scratch_shapes=[pltpu.VMEM((G, BUF_ROWS + T, HEAD_DIM), F32)],
        compiler_params=_cparams("parallel", "parallel"),
        name="conv_dn",
    )(u3, bufpad, w)


def _conv_lin_kernel(u_ref, buf_ref, w_ref, b_ref, o_ref, ext_ref):
    y = _conv_taps(u_ref, buf_ref, w_ref, ext_ref, LRU_CONV)
    o_ref[...] = y + b_ref[...][None]


def _conv_lin(u3, col0, bufpad, w, bias, *, G, tc):
    B, T, _ = u3.shape
    C = w.shape[1]
    return pl.pallas_call(
        _conv_lin_kernel,
        grid=(B // G, C // tc),
        in_specs=[pl.BlockSpec((G, T, tc), lambda b, c: (b, 0, col0 + c)),
                  pl.BlockSpec((G, BUF_ROWS, tc), lambda b, c: (b, 0, c)),
                  pl.BlockSpec((LRU_CONV, tc), lambda b, c: (0, c)),
                  pl.BlockSpec((1, tc), lambda b, c: (0, c))],
        out_specs=pl.BlockSpec((G, T, tc), lambda b, c: (b, 0, c)),
        out_shape=jax.ShapeDtypeStruct((B, T, C), F32),
        scratch_shapes=[pltpu.VMEM((G, BUF_ROWS + T, tc), F32)],
        compiler_params=_cparams("parallel", "parallel"),
        name="conv_lin",
    )(u3, bufpad, w, bias.reshape(1, C))


def _ffn_kernel(x_ref, wg_ref, wv_ref, cwg_ref, cwv_ref, cbg_ref, cbv_ref, hg_ref, hv_ref, wd_ref,
                g_ref, b_ref, y_ref, tg_ref, tv_ref, xb_scr, acc_scr, ext_scr, carry_scr,
                *, G, Tt, nj, tiles_per_seq):
    i = pl.program_id(0)
    j = pl.program_id(1)
    tm, tf = G * Tt, ext_scr.shape[2]

    @pl.when(j == 0)
    def _():
        xb_scr[...] = x_ref[...].astype(BF16)

    xb = xb_scr[...]
    halves = []
    for s, (w_ref, cw_ref, cb_ref, h_ref, t_ref) in enumerate(
            ((wg_ref, cwg_ref, cbg_ref, hg_ref, tg_ref), (wv_ref, cwv_ref, cbv_ref, hv_ref, tv_ref))):
        u = _dot(xb, w_ref[0])
        if G == 1:
            first = (i % tiles_per_seq) == 0

            @pl.when(first)
            def _():
                ext_scr[0, 0:BUF_ROWS, :] = h_ref[0]

            @pl.when(jnp.logical_not(first))
            def _():
                ext_scr[0, 0:BUF_ROWS, :] = carry_scr[j, s]

            tail = u[tm - BUF_ROWS:, :]
            carry_scr[j, s] = tail
            t_ref[0] = tail
        else:
            t_ref[...] = u.reshape(G, Tt, tf)
            ext_scr[:, 0:BUF_ROWS, :] = h_ref[...]
        ext_scr[:, BUF_ROWS:BUF_ROWS + Tt, :] = u.reshape(G, Tt, tf)
        y = None
        for t in range(FFN_CONV):
            off = BUF_ROWS - (FFN_CONV - 1) + t
            term = ext_scr[:, off:off + Tt, :] * cw_ref[t:t + 1, :][None]
            y = term if y is None else y + term
        halves.append(y + cb_ref[...][None])
    yg, yv = halves
    h = (yg * _sigmoid(yg) * yv).reshape(tm, tf).astype(BF16)
    part = _dot(h, wd_ref[...])

    @pl.when(j == 0)
    def _():
        acc_scr[...] = part

    @pl.when(j > 0)
    def _():
        acc_scr[...] += part

    @pl.when(j == nj - 1)
    def _():
        hsum = DEEPNORM_ALPHA * x_ref[...] + acc_scr[...]
        hc = hsum - jnp.mean(hsum, axis=-1, keepdims=True)
        var = jnp.mean(hc * hc, axis=-1, keepdims=True)
        y_ref[...] = hc * lax.rsqrt(var + LN_EPS) * g_ref[...] + b_ref[...]


def _ffn(x, wup_b, conv_w, conv_b, hist, w_down, g, b, *, B, T, tm):
    M, D = x.shape
    nj2, _, tf = wup_b.shape
    nj = nj2 // 2
    ffp = nj * tf
    long_seq = T >= tm
    G, Tt = (1, tm) if long_seq else (tm // T, T)
    assert M % tm == 0 and T % Tt == 0 and (long_seq or Tt == BUF_ROWS)
    tiles_per_seq = T // Tt if long_seq else 1
    seq = (lambda i: i // tiles_per_seq) if long_seq else (lambda i: i)
    col = lambda half: (lambda i, j: (0, half * nj + j))
    hspec = lambda half: pl.BlockSpec((G, BUF_ROWS, tf), lambda i, j: (seq(i), 0, half * nj + j))
    tspec = pl.BlockSpec((G, BUF_ROWS, tf), lambda i, j: (i, 0, j))
    tshape = jax.ShapeDtypeStruct((M // tm * G, BUF_ROWS, ffp), F32)
    y, tail_g, tail_v = pl.pallas_call(
        functools.partial(_ffn_kernel, G=G, Tt=Tt, nj=nj, tiles_per_seq=tiles_per_seq),
        grid=(M // tm, nj),
        in_specs=[pl.BlockSpec((tm, D), lambda i, j: (i, 0)),
                  pl.BlockSpec((1, D, tf), lambda i, j: (j, 0, 0)),
                  pl.BlockSpec((1, D, tf), lambda i, j: (nj + j, 0, 0)),
                  pl.BlockSpec((FFN_CONV, tf), col(0)),
                  pl.BlockSpec((FFN_CONV, tf), col(1)),
                  pl.BlockSpec((1, tf), col(0)),
                  pl.BlockSpec((1, tf), col(1)),
                  hspec(0), hspec(1),
                  pl.BlockSpec((tf, D), lambda i, j: (j, 0)),
                  pl.BlockSpec((1, D), lambda i, j: (0, 0)),
                  pl.BlockSpec((1, D), lambda i, j: (0, 0))],
        out_specs=[pl.BlockSpec((tm, D), lambda i, j: (i, 0)), tspec, tspec],
        out_shape=[jax.ShapeDtypeStruct((M, D), F32), tshape, tshape],
        scratch_shapes=[pltpu.VMEM((tm, D), BF16),
                        pltpu.VMEM((tm, D), F32),
                        pltpu.VMEM((G, BUF_ROWS + Tt, tf), F32),
                        pltpu.VMEM((nj, 2, BUF_ROWS, tf), F32)],
        compiler_params=_cparams("arbitrary", "arbitrary"),
        name="conv_ffn",
    )(x, wup_b, wup_b, conv_w, conv_w, conv_b.reshape(1, 2 * ffp), conv_b.reshape(1, 2 * ffp),
      hist, hist, w_down, g.reshape(1, D), b.reshape(1, D))
    last = slice(tiles_per_seq - 1, None, tiles_per_seq)
    return y, tail_g[last], tail_v[last]


def _lane_col(x, lane_iota, idx):
    return jnp.sum(jnp.where(lane_iota == idx, x, 0.0), axis=1, keepdims=True)


def _delta_kernel(q_ref, k_ref, v_ref, gate_ref, ab_ref, s0_ref, alog_ref, dtb_ref, nw_ref,
                  o_ref, sout_ref, s_scr, *, C, nchunks):
    n = pl.program_id(1)

    @pl.when(n == 0)
    def _():
        s_scr[...] = s0_ref[0]

    H = DN_HEADS
    gh = min(H, LANES // C)
    R = gh * C
    groups = [list(range(g * gh, (g + 1) * gh)) for g in range(H // gh)]
    hsl = lambda h: slice(h * HEAD_DIM, (h + 1) * HEAD_DIM)
    rsl = lambda j: slice(j * C, (j + 1) * C)
    cat0 = lambda xs: xs[0] if len(xs) == 1 else jnp.concatenate(xs, axis=0)

    row = lax.broadcasted_iota(jnp.int32, (R, R), 0)
    col = lax.broadcasted_iota(jnp.int32, (R, R), 1)
    same = (row // C) == (col // C)
    eye = row == col
    incl = same & (row >= col)
    strict = same & (row > col)
    eyef = jnp.where(eye, 1.0, 0.0).astype(F32)
    lane = lax.broadcasted_iota(jnp.int32, (C, LANES), 1)
    rowhead = lax.broadcasted_iota(jnp.int32, (R, HEAD_DIM), 0) // C

    ab = ab_ref[...]
    z = ab + dtb_ref[...]
    softplus = jnp.maximum(z, 0.0) + jnp.log1p(jnp.exp(-jnp.abs(z)))
    g_all = -jnp.exp(alog_ref[...]) * softplus
    beta_all = _sigmoid(ab)
    trow = lax.broadcasted_iota(jnp.int32, (C, C), 0)
    tcol = lax.broadcasted_iota(jnp.int32, (C, C), 1)
    tril = jnp.where(trow >= tcol, 1.0, 0.0).astype(F32)
    if C >= 2 * SUBLANES:
        gc_all = _dot3(tril, g_all)
    else:
        gc_all = jnp.dot(tril, g_all, preferred_element_type=F32, precision=lax.Precision.HIGHEST)

    q = [cat0([q_ref[:, hsl(h)] for h in hs]) for hs in groups]
    k = [cat0([k_ref[:, hsl(h)] for h in hs]) for hs in groups]
    v = [cat0([v_ref[:, hsl(h)] for h in hs]) for hs in groups]
    gcol_h = [_lane_col(gc_all, lane, h) for h in range(H)]
    gcol = [cat0([gcol_h[h] for h in hs]) for hs in groups]
    bcol = [cat0([_lane_col(beta_all, lane, H + h) for h in hs]) for hs in groups]
    glast_h = [g[C - 1:C, :] for g in gcol_h]
    glast = [cat0([jnp.broadcast_to(glast_h[h], (C, 1)) for h in hs]) for hs in groups]
    grow = [jnp.sum(jnp.where(eye, g, 0.0), axis=0, keepdims=True) for g in gcol]
    decay = [jnp.where(incl, jnp.exp(jnp.where(incl, gc - gr, 0.0)), 0.0) for gc, gr in zip(gcol, grow)]
    kb = [ki * bi for ki, bi in zip(k, bcol)]
    kq = [_dot_nt(jnp.concatenate([kbi, qi], axis=0).astype(BF16), ki.astype(BF16))
          for kbi, qi, ki in zip(kb, q, k)]
    m = [jnp.where(strict, x[:R] * d, 0.0) for x, d in zip(kq, decay)]
    attn = [x[R:] * d for x, d in zip(kq, decay)]

    nm = [-x for x in m]
    p = [eyef + x for x in nm]
    nk = [_dot3(x, x) for x in nm]
    terms = 2
    while terms * 2 < C:
        res = [_dot3(jnp.concatenate([pi, ni], axis=0), ni) for pi, ni in zip(p, nk)]
        p = [pi + r[:R] for pi, r in zip(p, res)]
        nk = [r[R:] for r in res]
        terms *= 2
    p = [pi + _dot3(pi, ni) for pi, ni in zip(p, nk)]

    eg = [jnp.exp(g) for g in gcol]
    uw = [_dot(pi.astype(BF16), jnp.concatenate([vi * bi, kbi * e], axis=1).astype(BF16))
          for pi, vi, bi, kbi, e in zip(p, v, bcol, kb, eg)]
    qg = [qi * e for qi, e in zip(q, eg)]
    kg = [ki * jnp.exp(gl - gc) for ki, gl, gc in zip(k, glast, gcol)]

    s_old = [s_scr[h] for h in range(H)]
    sb = [s.astype(BF16) for s in s_old]
    ws = []
    for g, hs in enumerate(groups):
        for j, h in enumerate(hs):
            lhs = jnp.concatenate([uw[g][rsl(j), HEAD_DIM:], qg[g][rsl(j), :]], axis=0).astype(BF16)
            ws.append(_dot(lhs, sb[h]))
    v_new = [cat0([uw[g][rsl(j), :HEAD_DIM] - ws[h][:C] for j, h in enumerate(hs)])
             for g, hs in enumerate(groups)]
    vnb = [x.astype(BF16) for x in v_new]
    o_all = [cat0([ws[h][C:] for h in hs]) + _dot(a.astype(BF16), vb)
             for hs, a, vb in zip(groups, attn, vnb)]
    for g, hs in enumerate(groups):
        for j, h in enumerate(hs):
            kgm = jnp.where(rowhead == j, kg[g], 0.0).astype(BF16) if gh > 1 else kg[g].astype(BF16)
            s_scr[h] = s_old[h] * jnp.exp(glast_h[h]) + _dot_tn(kgm, vnb[g])
    for g, hs in enumerate(groups):
        for j, h in enumerate(hs):
            o = o_all[g][rsl(j), :]
            o = o * lax.rsqrt(jnp.mean(o * o, axis=-1, keepdims=True) + NORM_EPS) * nw_ref[...]
            gate = gate_ref[:, hsl(h)]
            o_ref[:, hsl(h)] = (o * (gate * _sigmoid(gate))).astype(o_ref.dtype)

    @pl.when(n == nchunks - 1)
    def _():
        sout_ref[0] = s_scr[...]


def _delta(qkvn, proj, ab, s0, a_log, dt_bias, norm_w, *, B, T, C):
    nchunks = T // C
    W = DN_HEADS * HEAD_DIM
    pad = LANES - DN_HEADS
    alog_p = jnp.pad(a_log.astype(F32), (0, pad)).reshape(1, LANES)
    dtb_p = jnp.pad(dt_bias.astype(F32), (0, pad)).reshape(1, LANES)
    rows = lambda b, n: b * nchunks + n
    return pl.pallas_call(
        functools.partial(_delta_kernel, C=C, nchunks=nchunks),
        grid=(B, nchunks),
        in_specs=[pl.BlockSpec((C, W), lambda b, n: (rows(b, n), 0)),
                  pl.BlockSpec((C, W), lambda b, n: (rows(b, n), 1)),
                  pl.BlockSpec((C, W), lambda b, n: (rows(b, n), 2)),
                  pl.BlockSpec((C, W), lambda b, n: (rows(b, n), 3)),
                  pl.BlockSpec((C, LANES), lambda b, n: (rows(b, n), 0)),
                  pl.BlockSpec((1, DN_HEADS, HEAD_DIM, HEAD_DIM), lambda b, n: (b, 0, 0, 0)),
                  pl.BlockSpec((1, LANES), lambda b, n: (0, 0)),
                  pl.BlockSpec((1, LANES), lambda b, n: (0, 0)),
                  pl.BlockSpec((1, HEAD_DIM), lambda b, n: (0, 0))],
        out_specs=[pl.BlockSpec((C, W), lambda b, n: (rows(b, n), 0)),
                   pl.BlockSpec((1, DN_HEADS, HEAD_DIM, HEAD_DIM), lambda b, n: (b, 0, 0, 0))],
        out_shape=[jax.ShapeDtypeStruct((B * T, W), BF16),
                   jax.ShapeDtypeStruct((B, DN_HEADS, HEAD_DIM, HEAD_DIM), F32)],
        scratch_shapes=[pltpu.VMEM((DN_HEADS, HEAD_DIM, HEAD_DIM), F32)],
        compiler_params=_cparams("parallel", "arbitrary"),
        name="delta_rule",
    )(qkvn, qkvn, qkvn, proj, ab, s0, alog_p, dtb_p, norm_w.reshape(1, HEAD_DIM))


def _heads_to_lanes(ref, nheads, ntok):
    return jnp.concatenate(
        [ref[pl.ds(h, ntok, stride=nheads), :].astype(BF16) for h in range(nheads)], axis=1)


def _sb_blocks(qb, kblks, vblks, bias, carry, masks, ustrict):
    zs = [_dot_nt(qb, kb) * ATT_SCALE + bias for kb in kblks]
    lks = [_neg_softplus(z) for z in zs]
    lss = [z + lk for z, lk in zip(zs, lks)]
    lks = [lk if mk is None else jnp.where(mk, lk, 0.0) for lk, mk in zip(lks, masks)]
    parts = [_split2(lk) for lk in lks]
    rs = [_dot(hi, ustrict) + _dot(lo, ustrict) for hi, lo in parts]
    sums = [jnp.sum(lk, axis=1, keepdims=True) for lk in lks]
    acc = None
    for ls, r, sm, mk, vb in zip(lss, rs, sums, masks, vblks):
        a = jnp.exp(ls + r + carry)
        if mk is not None:
            a = jnp.where(mk, a, 0.0)
        pv = _dot(a.astype(BF16), vb)
        acc = pv if acc is None else acc + pv
        carry = carry + sm
    return acc, carry


def _sbp_kernel(bias_ref, q_ref, k_ref, v_ref, o_ref, kb_scr, vb_scr, *, tq):
    h = pl.program_id(1)
    qi = pl.program_id(2)

    @pl.when(qi == 0)
    def _():
        kb_scr[...] = k_ref[...].astype(BF16)
        vb_scr[...] = v_ref[...].astype(BF16)

    qb = q_ref[...].astype(BF16)
    bias = bias_ref[h]
    row = lax.broadcasted_iota(jnp.int32, (tq, tq), 0)
    col = lax.broadcasted_iota(jnp.int32, (tq, tq), 1)
    ustrict = jnp.where(row > col, 1.0, 0.0).astype(BF16)

    def body(it, st):
        acc, carry = st
        kblks, vblks, masks = [], [], []
        for d in range(2):
            kj = qi - 2 * it - d
            off = jnp.where(kj >= 0, (qi - kj) * tq, -tq)
            ks = pl.multiple_of(jnp.maximum(kj, 0) * tq, tq)
            kblks.append(kb_scr[pl.ds(ks, tq), :])
            vblks.append(vb_scr[pl.ds(ks, tq), :])
            masks.append(col < row + off)
        pv, carry = _sb_blocks(qb, kblks, vblks, bias, carry, masks, ustrict)
        return acc + pv, carry

    init = (jnp.zeros((tq, HEAD_DIM), F32), jnp.zeros((tq, 1), F32))
    acc, _ = lax.fori_loop(0, qi // 2 + 1, body, init)
    o_ref[...] = acc.astype(o_ref.dtype)


def _sb_prompt(proj, sb_bias, *, B, T, qcol, kcol, vcol, tq):
    nq = T // tq
    grid_spec = pltpu.PrefetchScalarGridSpec(
        num_scalar_prefetch=1,
        grid=(B, SB_HEADS, nq),
        in_specs=[pl.BlockSpec((tq, HEAD_DIM), lambda b, h, i, s: (b * nq + i, qcol + h)),
                  pl.BlockSpec((T, HEAD_DIM), lambda b, h, i, s: (b, kcol + h)),
                  pl.BlockSpec((T, HEAD_DIM), lambda b, h, i, s: (b, vcol + h))],
        out_specs=pl.BlockSpec((tq, HEAD_DIM), lambda b, h, i, s: (b * nq + i, h)),
        scratch_shapes=[pltpu.VMEM((T, HEAD_DIM), BF16), pltpu.VMEM((T, HEAD_DIM), BF16)],
    )
    return pl.pallas_call(
        functools.partial(_sbp_kernel, tq=tq),
        grid_spec=grid_spec,
        out_shape=jax.ShapeDtypeStruct((B * T, SB_HEADS * HEAD_DIM), BF16),
        compiler_params=_cparams("parallel", "parallel", "arbitrary"),
        name="sb_prompt",
    )(sb_bias.astype(F32), proj, proj, proj)


def _sbs_kernel(pt_ref, q_ref, kn_ref, vn_ref, *rest, T, nsteps, pps):
    kc_refs, vc_refs = rest[:pps], rest[pps:2 * pps]
    bias_ref, o_ref, qbd_scr, knp_scr, vnp_scr, acc_scr, carry_scr = rest[2 * pps:]
    p = pl.program_id(1)
    R = SB_HEADS * T
    W = SB_HEADS * HEAD_DIM
    rrow = lax.broadcasted_iota(jnp.int32, (R, W), 0)
    rcol = lax.broadcasted_iota(jnp.int32, (R, W), 1)
    own_head = (rrow // T) == (rcol // HEAD_DIM)
    row = lax.broadcasted_iota(jnp.int32, (PAGE_SIZE, PAGE_SIZE), 0)
    col = lax.broadcasted_iota(jnp.int32, (PAGE_SIZE, PAGE_SIZE), 1)
    ustrict = jnp.where(row > col, 1.0, 0.0).astype(BF16)
    bias = bias_ref[...]

    def step(kblks, vblks, masks):
        pv, carry = _sb_blocks(qbd_scr[...], kblks, vblks, bias, carry_scr[...], masks, ustrict)
        acc_scr[...] += pv
        carry_scr[...] = carry

    @pl.when(p == 0)
    def _():
        q = q_ref[...]
        qrep = jnp.concatenate([q] * SB_HEADS, axis=0)
        qbd_scr[...] = jnp.where(own_head, qrep, 0.0).astype(BF16)
        knp_scr[...] = jnp.zeros_like(knp_scr)
        vnp_scr[...] = jnp.zeros_like(vnp_scr)
        knp_scr[0:T, :] = kn_ref[...]
        vnp_scr[0:T, :] = vn_ref[...]
        acc_scr[...] = jnp.zeros_like(acc_scr)
        carry_scr[...] = jnp.zeros_like(carry_scr)
        qpos = lax.broadcasted_iota(jnp.int32, (R, PAGE_SIZE), 0) % T
        kpos = lax.broadcasted_iota(jnp.int32, (R, PAGE_SIZE), 1)
        step([knp_scr[...].astype(BF16)], [vnp_scr[...].astype(BF16)], [kpos < qpos])

    step([_heads_to_lanes(r.at[0], SB_HEADS, PAGE_SIZE) for r in kc_refs],
         [_heads_to_lanes(r.at[0], SB_HEADS, PAGE_SIZE) for r in vc_refs], [None] * pps)

    @pl.when(p == nsteps - 1)
    def _():
        accm = jnp.where(own_head, acc_scr[...], 0.0)
        out = accm[0:T]
        for h in range(1, SB_HEADS):
            out = out + accm[h * T:(h + 1) * T]
        o_ref[...] = out.astype(o_ref.dtype)


def _sb_sample(proj, cache_k, cache_v, page_table, sb_bias, *, B, T, qcol, kcol, vcol):
    npages = page_table.shape[1]
    pps = SB_PAGES_PER_STEP if npages % SB_PAGES_PER_STEP == 0 else 1
    nsteps = npages // pps
    W = SB_HEADS * HEAD_DIM
    R = SB_HEADS * T
    bias_rows = jnp.broadcast_to(jnp.repeat(sb_bias.astype(F32), T)[:, None], (R, PAGE_SIZE))

    def page_spec(i):
        return pl.BlockSpec((1, PAGE_SIZE * SB_HEADS, HEAD_DIM),
                            lambda b, p, pt: (pt[b, npages - 1 - (p * pps + i)], 0, 0))

    grid_spec = pltpu.PrefetchScalarGridSpec(
        num_scalar_prefetch=1,
        grid=(B, nsteps),
        in_specs=[pl.BlockSpec((T, W), lambda b, p, pt: (b, qcol)),
                  pl.BlockSpec((T, W), lambda b, p, pt: (b, kcol)),
                  pl.BlockSpec((T, W), lambda b, p, pt: (b, vcol))]
                 + [page_spec(i) for i in range(pps)] * 2
                 + [pl.BlockSpec((R, PAGE_SIZE), lambda b, p, pt: (0, 0))],
        out_specs=pl.BlockSpec((T, W), lambda b, p, pt: (b, 0)),
        scratch_shapes=[pltpu.VMEM((R, W), BF16),
                        pltpu.VMEM((PAGE_SIZE, W), F32),
                        pltpu.VMEM((PAGE_SIZE, W), F32),
                        pltpu.VMEM((R, W), F32),
                        pltpu.VMEM((R, 1), F32)],
    )
    return pl.pallas_call(
        functools.partial(_sbs_kernel, T=T, nsteps=nsteps, pps=pps),
        grid_spec=grid_spec,
        out_shape=jax.ShapeDtypeStruct((B * T, W), BF16),
        compiler_params=_cparams("arbitrary", "arbitrary"),
        name="sb_sample",
    )(page_table, proj, proj, proj, *([cache_k] * pps), *([cache_v] * pps), bias_rows)


def _xattn_prompt_kernel(q_ref, k_ref, v_ref, o_ref):
    for h in range(X_HEADS):
        sl = slice(h * HEAD_DIM, (h + 1) * HEAD_DIM)
        s = _dot_nt(q_ref[:, sl].astype(BF16), k_ref[0, :, sl].astype(BF16)) * ATT_SCALE
        p = jnp.exp(s - jnp.max(s, axis=-1, keepdims=True))
        p = p / jnp.sum(p, axis=-1, keepdims=True)
        o_ref[:, sl] = _dot(p.astype(BF16), v_ref[0, :, sl].astype(BF16)).astype(o_ref.dtype)


def _xattn_prompt(q, mk, mv, *, B, T, tq):
    W = X_HEADS * HEAD_DIM
    nq = T // tq
    nm = mk.shape[1]
    return pl.pallas_call(
        _xattn_prompt_kernel,
        grid=(B, nq),
        in_specs=[pl.BlockSpec((tq, W), lambda b, i: (b * nq + i, 0)),
                  pl.BlockSpec((1, nm, W), lambda b, i: (b, 0, 0)),
                  pl.BlockSpec((1, nm, W), lambda b, i: (b, 0, 0))],
        out_specs=pl.BlockSpec((tq, W), lambda b, i: (b * nq + i, 0)),
        out_shape=jax.ShapeDtypeStruct((B * T, W), BF16),
        compiler_params=_cparams("parallel", "parallel"),
        name="xattn_prompt",
    )(q, mk, mv)


def _xattn_sample_kernel(q_ref, k_ref, v_ref, o_ref, *, G, T, nm):
    R = X_HEADS * T
    W = X_HEADS * HEAD_DIM
    rrow = lax.broadcasted_iota(jnp.int32, (R, W), 0)
    rcol = lax.broadcasted_iota(jnp.int32, (R, W), 1)
    own_head = (rrow // T) == (rcol // HEAD_DIM)
    for g in range(G):
        q = q_ref[g]
        qbd = jnp.where(own_head, jnp.concatenate([q] * X_HEADS, axis=0), 0.0).astype(BF16)
        s = _dot_nt(qbd, _heads_to_lanes(k_ref.at[g], X_HEADS, nm)) * ATT_SCALE
        p = jnp.exp(s - jnp.max(s, axis=-1, keepdims=True))
        p = p / jnp.sum(p, axis=-1, keepdims=True)
        of = jnp.where(own_head, _dot(p.astype(BF16), _heads_to_lanes(v_ref.at[g], X_HEADS, nm)), 0.0)
        out = of[0:T]
        for h in range(1, X_HEADS):
            out = out + of[h * T:(h + 1) * T]
        o_ref[g] = out.astype(o_ref.dtype)


def _xattn_sample(q3, mk, mv, *, G):
    B, T, W = q3.shape
    nm = mk.shape[1] // X_HEADS
    return pl.pallas_call(
        functools.partial(_xattn_sample_kernel, G=G, T=T, nm=nm),
        grid=(B // G,),
        in_specs=[pl.BlockSpec((G, T, W), lambda b: (b, 0, 0)),
                  pl.BlockSpec((G, nm * X_HEADS, HEAD_DIM), lambda b: (b, 0, 0)),
                  pl.BlockSpec((G, nm * X_HEADS, HEAD_DIM), lambda b: (b, 0, 0))],
        out_specs=pl.BlockSpec((G, T, W), lambda b: (b, 0, 0)),
        out_shape=jax.ShapeDtypeStruct((B, T, W), BF16),
        compiler_params=_cparams("parallel"),
        name="xattn_sample",
    )(q3, mk, mv)


def _lru_kernel(y_ref, xc_ref, wr_ref, wi_ref, br_ref, bi_ref, lam_ref, h0_ref, o_ref, hl_ref,
                hc_scr, a_scr, b_scr, h_scr, *, G, Tt, nt):
    t = pl.program_id(2)
    rows = G * Tt
    ngroups = rows // SUBLANES
    tc = a_scr.shape[1]

    @pl.when(t == 0)
    def _():
        hc_scr[...] = h0_ref[...]

    xc = xc_ref[...].reshape(rows, tc)
    xb = xc.astype(BF16)
    r = _sigmoid(_dot(xb, wr_ref[0]) + br_ref[...])
    i = _sigmoid(_dot(xb, wi_ref[0]) + bi_ref[...])
    log_a = LRU_C * r * _neg_softplus(-lam_ref[...])
    a = jnp.exp(log_a)
    a_scr[...] = a
    b_scr[...] = jnp.sqrt(-jnp.tanh(log_a) * (a * a + 1.0)) * (i * xc)
    sub = lax.broadcasted_iota(jnp.int32, (SUBLANES, tc), 0)

    def group(gi, carry):
        r0 = pl.multiple_of(gi * SUBLANES, SUBLANES)
        a = a_scr[pl.ds(r0, SUBLANES), :]
        b = b_scr[pl.ds(r0, SUBLANES), :]
        for s in (1, 2, 4):
            keep = sub >= s
            a_sh = jnp.where(keep, pltpu.roll(a, s, 0), 1.0)
            b_sh = jnp.where(keep, pltpu.roll(b, s, 0), 0.0)
            b = a * b_sh + b
            a = a * a_sh
        hprev = hc_scr[gi] if Tt == SUBLANES else carry
        h8 = b + a * hprev
        h_scr[pl.ds(r0, SUBLANES), :] = h8
        last = h8[SUBLANES - 1:SUBLANES, :]
        if Tt == SUBLANES:
            hl_ref[gi] = last
            return carry
        return last

    carry = lax.fori_loop(0, ngroups, group, hc_scr[0])
    if Tt != SUBLANES:
        hc_scr[0] = carry

        @pl.when(t == nt - 1)
        def _():
            hl_ref[0] = carry

    y = y_ref[...].reshape(rows, tc)
    o_ref[...] = (jax.nn.gelu(y) * h_scr[...]).astype(o_ref.dtype).reshape(G, Tt, tc)


def _lru(proj3, xc3, w_r, w_i, b_r, b_i, lam, h0, *, G, Tt):
    B, T, D = xc3.shape
    tc = D // LRU_BLOCKS
    nt = T // Tt
    assert Tt % SUBLANES == 0 and (G == 1 or Tt == SUBLANES)
    vec = lambda a: a.astype(F32).reshape(1, D)
    return pl.pallas_call(
        functools.partial(_lru_kernel, G=G, Tt=Tt, nt=nt),
        grid=(B // G, LRU_BLOCKS, nt),
        in_specs=[pl.BlockSpec((G, Tt, tc), lambda b, c, t: (b, t, c)),
                  pl.BlockSpec((G, Tt, tc), lambda b, c, t: (b, t, c)),
                  pl.BlockSpec((1, tc, tc), lambda b, c, t: (c, 0, 0)),
                  pl.BlockSpec((1, tc, tc), lambda b, c, t: (c, 0, 0)),
                  pl.BlockSpec((1, tc), lambda b, c, t: (0, c)),
                  pl.BlockSpec((1, tc), lambda b, c, t: (0, c)),
                  pl.BlockSpec((1, tc), lambda b, c, t: (0, c)),
                  pl.BlockSpec((G, 1, tc), lambda b, c, t: (b, 0, c))],
        out_specs=[pl.BlockSpec((G, Tt, tc), lambda b, c, t: (b, t, c)),
                   pl.BlockSpec((G, 1, tc), lambda b, c, t: (b, 0, c))],
        out_shape=[jax.ShapeDtypeStruct((B, T, D), BF16),
                   jax.ShapeDtypeStruct((B, 1, D), F32)],
        scratch_shapes=[pltpu.VMEM((G, 1, tc), F32),
                        pltpu.VMEM((G * Tt, tc), F32),
                        pltpu.VMEM((G * Tt, tc), F32),
                        pltpu.VMEM((G * Tt, tc), F32)],
        compiler_params=_cparams("parallel", "parallel", "arbitrary"),
        name="rglru",
    )(proj3, xc3, w_r, w_i, vec(b_r), vec(b_i), vec(lam), h0.reshape(B, 1, D))


def _pad_buf(buf):
    return jnp.pad(buf, ((0, 0), (BUF_ROWS - buf.shape[1], 0), (0, 0)))


def _row_tile(m):
    return 512 if m % 512 == 0 else m


def _col_tile(n):
    return 1024 if n % 1024 == 0 else 512 if n % 512 == 0 else n


def _wblocks(w):
    return _block_cols(w.astype(BF16), _col_tile(w.shape[1]))


class _Stream:
    def __init__(self, B, T):
        self.B, self.T = B, T
        self.M = B * T
        self.long = T > SUBLANES
        self.G = 1 if self.long else B


def _ab_layer(st, x, w, dn_conv, dn_S, sb_fn):
    B, T, M = st.B, st.T, st.M
    tm = _row_tile(M)
    proj = _mm(x, w["ab_main"], tm=tm)
    ab = _mm(x, w["ab_ab"], tm=tm)
    proj3 = proj.reshape(B, T, -1)
    nqkv = 3 * DN_HEADS * HEAD_DIM
    qkvn = _conv_dn(proj3, _pad_buf(dn_conv), w["ab_conv_w"], G=st.G)
    C = 64 if st.long else T
    o_dn, s_new = _delta(qkvn.reshape(M, nqkv), proj, ab, dn_S, w["dn_A_log"], w["dn_dt_bias"],
                         w["dn_norm_w"], B=B, T=T, C=C)
    o_sb = sb_fn(proj)
    o = jnp.concatenate([o_dn, o_sb], axis=-1)
    new_conv = proj3[:, T - (DN_CONV - 1):, :nqkv]
    W = SB_HEADS * HEAD_DIM
    k_new = proj[:, 5 * W:6 * W].reshape(B, T, SB_HEADS, HEAD_DIM)
    v_new = proj[:, 6 * W:7 * W].reshape(B, T, SB_HEADS, HEAD_DIM)
    return o, w["ab_w_out"], (k_new, v_new, new_conv, s_new)


def _lru_layer(st, x, w, lru_conv, lru_h):
    B, T, M = st.B, st.T, st.M
    D = x.shape[1]
    proj = _mm(x, w["c_w_in"], tm=_row_tile(M))
    proj3 = proj.reshape(B, T, 2 * D)
    tc = 256
    xc3 = _conv_lin(proj3, D // tc, _pad_buf(lru_conv), w["lru_conv_w"], w["lru_conv_b"], G=st.G, tc=tc)
    Tt = 512 if st.long else T
    o3, h_last = _lru(proj3, xc3, w["lru_w_r"], w["lru_w_i"], w["lru_b_r"], w["lru_b_i"],
                      w["lru_lambda"], lru_h, G=st.G, Tt=Tt)
    new_conv = proj3[:, T - (LRU_CONV - 1):, D:]
    return o3.reshape(M, D), w["c_w_out"], (new_conv, h_last.reshape(B, D))


def _xattn_layer(st, x, w, mk3, mv3):
    B, T, M = st.B, st.T, st.M
    q = _mm(x, w["xa_w_q"], tm=_row_tile(M))
    if st.long:
        return _xattn_prompt(q, mk3, mv3, B=B, T=T, tq=512)
    W = q.shape[1]
    return _xattn_sample(q.reshape(B, T, W), mk3, mv3, G=8).reshape(M, W)


def _ffn_layer(st, x, w, ffn_conv, d_ff, g, b):
    B, T, M = st.B, st.T, st.M
    ffp = w["ffn_conv_w"].shape[1] // 2
    buf = ffn_conv
    bufp = jnp.concatenate([jnp.pad(buf[..., :d_ff], ((0, 0), (0, 0), (0, ffp - d_ff))),
                            jnp.pad(buf[..., d_ff:], ((0, 0), (0, 0), (0, ffp - d_ff)))], axis=-1)
    y, tail_g, tail_v = _ffn(x, w["ffn_w_up"], w["ffn_conv_w"], w["ffn_conv_b"], _pad_buf(bufp),
                             w["ffn_w_down"], g, b, B=B, T=T, tm=_row_tile(M))
    keep = slice(BUF_ROWS - (FFN_CONV - 1), BUF_ROWS)
    new_conv = jnp.concatenate([tail_g[:, keep, :d_ff], tail_v[:, keep, :d_ff]], axis=-1)
    return y, new_conv


def kernel(x_prompt, x_sample, mem_prompt, cache_sb_k, cache_sb_v, cache_mem_k, cache_mem_v, state_dn_conv, state_dn_S, state_lru_conv, state_lru_h, state_ffn_conv, page_table, ab_w_in, ab_conv_w, dn_A_log, dn_dt_bias, dn_norm_w, sb_bias, ab_w_out, c_w_in, lru_conv_w, lru_conv_b, lru_w_r, lru_b_r, lru_w_i, lru_b_i, lru_lambda, c_w_out, xa_w_q, xa_w_k, xa_w_v, xa_w_o, ffn_w_up, ffn_conv_w, ffn_conv_b, ffn_w_down, ln_g, ln_b):
    bp, tp, d_model = x_prompt.shape
    bs, ts, _ = x_sample.shape
    sp, ss = _Stream(bp, tp), _Stream(bs, ts)
    dt = x_prompt.dtype
    n_mem = mem_prompt.shape[1]
    xw = X_HEADS * HEAD_DIM
    nqkv = 3 * DN_HEADS * HEAD_DIM
    d_ff = ffn_w_down.shape[1]
    ffp = -(-d_ff // FFN_COL_TILE) * FFN_COL_TILE
    depth = ln_g.shape[0]

    xp = x_prompt.reshape(sp.M, d_model)
    xs = x_sample.reshape(ss.M, d_model)
    mem2 = mem_prompt.reshape(bp * n_mem, d_model)

    outs_p = {k: [] for k in ("sb_k", "sb_v", "dn_conv", "dn_S", "lru_conv", "lru_h", "mem_k", "mem_v", "ffn_conv")}
    outs_s = {k: [] for k in ("sb_k", "sb_v", "dn_conv", "dn_S", "lru_conv", "lru_h", "ffn_conv")}

    def pad_ff(a, axis):
        pad = [(0, 0)] * a.ndim
        pad[axis] = (0, ffp - d_ff)
        return jnp.pad(a, pad)

    for layer in range(depth):
        w = {}
        if layer % 2 == 0:
            e = layer // 2
            w_in = ab_w_in[e]
            c_ab = nqkv + DN_HEADS * HEAD_DIM
            w["ab_main"] = _wblocks(jnp.concatenate([w_in[:, :c_ab], w_in[:, c_ab + 2 * DN_HEADS:]], axis=1))
            w["ab_ab"] = _wblocks(jnp.pad(w_in[:, c_ab:c_ab + 2 * DN_HEADS], ((0, 0), (0, LANES - 2 * DN_HEADS))))
            w["ab_conv_w"] = ab_conv_w[e]
            w["dn_A_log"], w["dn_dt_bias"], w["dn_norm_w"] = dn_A_log[e], dn_dt_bias[e], dn_norm_w[e]
            w["ab_w_out"] = ab_w_out[e].astype(BF16)
        else:
            o = layer // 2
            w["c_w_in"] = _wblocks(c_w_in[o])
            w["lru_conv_w"], w["lru_conv_b"] = lru_conv_w[o], lru_conv_b[o]
            w["lru_w_r"], w["lru_w_i"] = lru_w_r[o].astype(BF16), lru_w_i[o].astype(BF16)
            w["lru_b_r"], w["lru_b_i"], w["lru_lambda"] = lru_b_r[o], lru_b_i[o], lru_lambda[o]
            w["c_w_out"] = c_w_out[o].astype(BF16)
        w["xa_w_q"] = _wblocks(xa_w_q[layer])
        w["ffn_w_up"] = _block_cols(jnp.concatenate([pad_ff(ffn_w_up[layer][:, :d_ff], 1),
                                                     pad_ff(ffn_w_up[layer][:, d_ff:], 1)], axis=1).astype(BF16),
                                    FFN_COL_TILE)
        w["ffn_conv_w"] = jnp.concatenate([pad_ff(ffn_conv_w[layer][:, :d_ff], 1),
                                           pad_ff(ffn_conv_w[layer][:, d_ff:], 1)], axis=1)
        w["ffn_conv_b"] = jnp.concatenate([pad_ff(ffn_conv_b[layer][:d_ff], 0),
                                           pad_ff(ffn_conv_b[layer][d_ff:], 0)], axis=0)
        w["ffn_w_down"] = pad_ff(ffn_w_down[layer], 0).astype(BF16)
        w_xo = xa_w_o[layer].astype(BF16)

        if layer % 2 == 0:
            e = layer // 2
            qcol, kcol, vcol = 4 * DN_HEADS, 5 * DN_HEADS, 6 * DN_HEADS
            sbp = functools.partial(_sb_prompt, sb_bias=sb_bias[e], B=bp, T=tp, qcol=qcol, kcol=kcol, vcol=vcol, tq=256)
            op, wout, (k_new, v_new, cb, s_new) = _ab_layer(
                sp, xp, w, jnp.zeros((bp, DN_CONV - 1, nqkv), dt),
                jnp.zeros((bp, DN_HEADS, HEAD_DIM, HEAD_DIM), F32), sbp)
            outs_p["sb_k"].append(k_new); outs_p["sb_v"].append(v_new)
            outs_p["dn_conv"].append(cb); outs_p["dn_S"].append(s_new)
            n_pool = cache_sb_k.shape[1]
            ck = cache_sb_k[e].reshape(n_pool, PAGE_SIZE * SB_HEADS, HEAD_DIM)
            cv = cache_sb_v[e].reshape(n_pool, PAGE_SIZE * SB_HEADS, HEAD_DIM)
            sbs = functools.partial(_sb_sample, cache_k=ck, cache_v=cv, page_table=page_table, sb_bias=sb_bias[e],
                                    B=bs, T=ts, qcol=4, kcol=5, vcol=6)
            os_, _, (k_new, v_new, cb, s_new) = _ab_layer(ss, xs, w, state_dn_conv[e], state_dn_S[e], sbs)
            outs_s["sb_k"].append(k_new); outs_s["sb_v"].append(v_new)
            outs_s["dn_conv"].append(cb); outs_s["dn_S"].append(s_new)
        else:
            o = layer // 2
            op, wout, (cb, hl) = _lru_layer(sp, xp, w, jnp.zeros((bp, LRU_CONV - 1, d_model), dt),
                                            jnp.zeros((bp, d_model), dt))
            outs_p["lru_conv"].append(cb); outs_p["lru_h"].append(hl)
            os_, _, (cb, hl) = _lru_layer(ss, xs, w, state_lru_conv[o], state_lru_h[o])
            outs_s["lru_conv"].append(cb); outs_s["lru_h"].append(hl)
        xp = _mm_ln(op, wout, xp, ln_g[layer, 0], ln_b[layer, 0], tm=256, tk=1024)
        xs = _mm_ln(os_, wout, xs, ln_g[layer, 0], ln_b[layer, 0], tm=256, tk=1024)

        mk = _mm(mem2, _wblocks(xa_w_k[layer]), tm=_row_tile(bp * n_mem))
        mv = _mm(mem2, _wblocks(xa_w_v[layer]), tm=_row_tile(bp * n_mem))
        outs_p["mem_k"].append(mk.reshape(bp, n_mem, X_HEADS, HEAD_DIM))
        outs_p["mem_v"].append(mv.reshape(bp, n_mem, X_HEADS, HEAD_DIM))
        ap = _xattn_layer(sp, xp, w, mk.reshape(bp, n_mem, xw), mv.reshape(bp, n_mem, xw))
        as_ = _xattn_layer(ss, xs, w, cache_mem_k[layer].reshape(bs, n_mem * X_HEADS, HEAD_DIM),
                           cache_mem_v[layer].reshape(bs, n_mem * X_HEADS, HEAD_DIM))
        xp = _mm_ln(ap, w_xo, xp, ln_g[layer, 1], ln_b[layer, 1], tm=256, tk=xw)
        xs = _mm_ln(as_, w_xo, xs, ln_g[layer, 1], ln_b[layer, 1], tm=256, tk=xw)

        xp, fb = _ffn_layer(sp, xp, w, jnp.zeros((bp, FFN_CONV - 1, 2 * d_ff), dt), d_ff,
                            ln_g[layer, 2], ln_b[layer, 2])
        outs_p["ffn_conv"].append(fb)
        xs, fb = _ffn_layer(ss, xs, w, state_ffn_conv[layer], d_ff, ln_g[layer, 2], ln_b[layer, 2])
        outs_s["ffn_conv"].append(fb)

    st = lambda xs_: jnp.stack(xs_)
    return (xp.reshape(bp, tp, d_model), xs.reshape(bs, ts, d_model),
            st(outs_p["sb_k"]), st(outs_p["sb_v"]), st(outs_p["dn_conv"]), st(outs_p["dn_S"]),
            st(outs_p["lru_conv"]), st(outs_p["lru_h"]), st(outs_p["mem_k"]), st(outs_p["mem_v"]),
            st(outs_p["ffn_conv"]),
            st(outs_s["sb_k"]), st(outs_s["sb_v"]), st(outs_s["dn_conv"]), st(outs_s["dn_S"]),
            st(outs_s["lru_conv"]), st(outs_s["lru_h"]), st(outs_s["ffn_conv"]))
```

```python
import functools

import jax
import jax.numpy as jnp
from jax import lax
from jax.experimental import pallas as pl
from jax.experimental.pallas import tpu as pltpu

F32 = jnp.float32
BF16 = jnp.bfloat16

HEAD_DIM = 128
DN_HEADS = 8
SB_HEADS = 8
X_HEADS = 4
DN_CONV = 4
LRU_CONV = 4
FFN_CONV = 3
LRU_BLOCKS = 8
LRU_C = 8.0
PAGE_SIZE = 128
DEPTH = 2
DEEPNORM_ALPHA = (2.0 * DEPTH) ** 0.25
LN_EPS = 1e-5
NORM_EPS = 1e-6
ATT_SCALE = HEAD_DIM ** -0.5

LANES = 128
SUBLANES = 8
VMEM_LIMIT_BYTES = 56 * 1024 * 1024
BUF_ROWS = SUBLANES
SB_PAGES_PER_STEP = 8
LN_ROW_TILE = 512
FFN_COL_TILE = 512
FFN_SUB_TILE = 256


def _cparams(*sem):
    return pltpu.CompilerParams(dimension_semantics=sem, vmem_limit_bytes=VMEM_LIMIT_BYTES)


def _dot(a, b):
    return jnp.dot(a, b, preferred_element_type=F32)


def _dot_nt(a, b):
    return lax.dot_general(a, b, (((1,), (1,)), ((), ())), preferred_element_type=F32)


def _dot_tn(a, b):
    return lax.dot_general(a, b, (((0,), (0,)), ((), ())), preferred_element_type=F32)


def _split2(x):
    hi = x.astype(BF16)
    lo = (x - hi.astype(F32)).astype(BF16)
    return hi, lo


def _dot3(a, b):
    ah, al = _split2(a)
    bh, bl = _split2(b)
    return _dot(ah, bh) + _dot(ah, bl) + _dot(al, bh)


def _sigmoid(x):
    return 1.0 / (1.0 + jnp.exp(-x))


def _neg_softplus(z):
    return jnp.minimum(-z, 0.0) - jnp.log1p(jnp.exp(-jnp.abs(z)))


def _mm_kernel(x_ref, w_ref, o_ref, xb_ref):
    @pl.when(pl.program_id(1) == 0)
    def _():
        xb_ref[...] = x_ref[...].astype(BF16)

    o_ref[...] = _dot(xb_ref[...], w_ref[...]).astype(o_ref.dtype)


def _mm(x, w, *, tm, out_dtype=F32):
    M, K = x.shape
    N = w.shape[1]
    tn = _col_tile(N)
    assert M % tm == 0 and w.shape[0] == K
    return pl.pallas_call(
        _mm_kernel,
        grid=(M // tm, N // tn),
        in_specs=[pl.BlockSpec((tm, K), lambda i, j: (i, 0)),
                  pl.BlockSpec((K, tn), lambda i, j: (0, j))],
        out_specs=pl.BlockSpec((tm, tn), lambda i, j: (i, j)),
        out_shape=jax.ShapeDtypeStruct((M, N), out_dtype),
        scratch_shapes=[pltpu.VMEM((tm, K), BF16)],
        compiler_params=_cparams("parallel", "arbitrary"),
        name="proj",
    )(x, w)


def _mm_ln_kernel(x_ref, w_ref, r_ref, g_ref, b_ref, o_ref, acc_ref, *, nk):
    k = pl.program_id(1)
    part = _dot(x_ref[...].astype(BF16), w_ref[...])

    @pl.when(k == 0)
    def _():
        acc_ref[...] = part

    @pl.when(k > 0)
    def _():
        acc_ref[...] += part

    @pl.when(k == nk - 1)
    def _():
        h = DEEPNORM_ALPHA * r_ref[...] + acc_ref[...]
        hc = h - jnp.mean(h, axis=-1, keepdims=True)
        var = jnp.mean(hc * hc, axis=-1, keepdims=True)
        o_ref[...] = hc * lax.rsqrt(var + LN_EPS) * g_ref[...] + b_ref[...]


def _mm_ln(x, w, resid, g, b, *, tm, tk):
    M, K = x.shape
    N = w.shape[1]
    tm = min(tm, M)
    assert M % tm == 0 and K % tk == 0
    nk = K // tk
    return pl.pallas_call(
        functools.partial(_mm_ln_kernel, nk=nk),
        grid=(M // tm, nk),
        in_specs=[pl.BlockSpec((tm, tk), lambda i, k: (i, k)),
                  pl.BlockSpec((tk, N), lambda i, k: (k, 0)),
                  pl.BlockSpec((tm, N), lambda i, k: (i, 0)),
                  pl.BlockSpec((1, N), lambda i, k: (0, 0)),
                  pl.BlockSpec((1, N), lambda i, k: (0, 0))],
        out_specs=pl.BlockSpec((tm, N), lambda i, k: (i, 0)),
        out_shape=jax.ShapeDtypeStruct((M, N), F32),
        scratch_shapes=[pltpu.VMEM((tm, N), F32)],
        compiler_params=_cparams("parallel", "arbitrary"),
        name="proj_ln",
    )(x, w, resid, g.reshape(1, N), b.reshape(1, N))


def _conv_taps(u_ref, buf_ref, w_ref, ext_ref, width):
    T = u_ref.shape[1]
    ext_ref[:, 0:BUF_ROWS, :] = buf_ref[...]
    ext_ref[:, BUF_ROWS:BUF_ROWS + T, :] = u_ref[...]
    y = None
    for j in range(width):
        off = BUF_ROWS - (width - 1) + j
        term = ext_ref[:, off:off + T, :] * w_ref[j:j + 1, :][None]
        y = term if y is None else y + term
    return y


def _conv_dn_kernel(u_ref, buf_ref, w_ref, o_ref, ext_ref):
    c = pl.program_id(1)
    y = _conv_taps(u_ref, buf_ref, w_ref, ext_ref, DN_CONV)
    y = y * _sigmoid(y)
    ss = jnp.sum(y * y, axis=-1, keepdims=True)
    scale = jnp.where(c < DN_HEADS, ATT_SCALE, 1.0).astype(F32)
    yn = y * lax.rsqrt(ss + NORM_EPS) * scale
    o_ref[...] = jnp.where(c < 2 * DN_HEADS, yn, y)


def _conv_dn(u3, bufpad, w, *, G):
    B, T, _ = u3.shape
    C = w.shape[1]
    nc = C // HEAD_DIM
    return pl.pallas_call(
        _conv_dn_kernel,
        grid=(B // G, nc),
        in_specs=[pl.BlockSpec((G, T, HEAD_DIM), lambda b, c: (b, 0, c)),
                  pl.BlockSpec((G, BUF_ROWS, HEAD_DIM), lambda b, c: (b, 0, c)),
                  pl.BlockSpec((DN_CONV, HEAD_DIM), lambda b, c: (0, c))],
        out_specs=pl.BlockSpec((G, T, HEAD_DIM), lambda b, c: (b, 0, c)),
        out_shape=jax.ShapeDtypeStruct((B, T, C), F32),
        scratch_shapes=[pltpu.VMEM((G, BUF_ROWS + T, HEAD_DIM), F32)],
        compiler_params=_cparams("parallel", "parallel"),
        name="conv_dn",
    )(u3, bufpad, w)


def _conv_lin_kernel(u_ref, buf_ref, w_ref, b_ref, o_ref, ext_ref):
    y = _conv_taps(u_ref, buf_ref, w_ref, ext_ref, LRU_CONV)
    o_ref[...] = y + b_ref[...][None]


def _conv_lin(u3, col0, bufpad, w, bias, *, G, tc):
    B, T, _ = u3.shape
    C = w.shape[1]
    return pl.pallas_call(
        _conv_lin_kernel,
        grid=(B // G, C // tc),
        in_specs=[pl.BlockSpec((G, T, tc), lambda b, c: (b, 0, col0 + c)),
                  pl.BlockSpec((G, BUF_ROWS, tc), lambda b, c: (b, 0, c)),
                  pl.BlockSpec((LRU_CONV, tc), lambda b, c: (0, c)),
                  pl.BlockSpec((1, tc), lambda b, c: (0, c))],
        out_specs=pl.BlockSpec((G, T, tc), lambda b, c: (b, 0, c)),
        out_shape=jax.ShapeDtypeStruct((B, T, C), F32),
        scratch_shapes=[pltpu.VMEM((G, BUF_ROWS + T, tc), F32)],
        compiler_params=_cparams("parallel", "parallel"),
        name="conv_lin",
    )(u3, bufpad, w, bias.reshape(1, C))


def _ffn_kernel(x_ref, wg_ref, wv_ref, cwg_ref, cwv_ref, cbg_ref, cbv_ref, hg_ref, hv_ref, wd_ref,
                g_ref, b_ref, y_ref, tg_ref, tv_ref, xb_scr, acc_scr, ext_scr, carry_scr,
                *, G, Tt, nj, tiles_per_seq):
    i = pl.program_id(0)
    j = pl.program_id(1)
    tm, tf = G * Tt, ext_scr.shape[3]
    sub = min(tf, FFN_SUB_TILE)

    @pl.when(j == 0)
    def _():
        xb_scr[...] = x_ref[...].astype(BF16)
        acc_scr[...] = jnp.zeros_like(acc_scr)

    if G == 1:
        @pl.when((i == 0) & (j == 0))
        def _():
            carry_scr[...] = jnp.zeros_like(carry_scr)

    xb = xb_scr[...]
    sides = ((wg_ref, cwg_ref, cbg_ref, hg_ref, tg_ref), (wv_ref, cwv_ref, cbv_ref, hv_ref, tv_ref))
    subs = [slice(c, c + sub) for c in range(0, tf, sub)]
    ups = [[_dot(xb, side[0][:, cs]) for side in sides] for cs in subs]
    first = (i % tiles_per_seq) == 0
    part = None
    for cs, us in zip(subs, ups):
        ys = []
        for s, (u, (_, cw_ref, cb_ref, h_ref, t_ref)) in enumerate(zip(us, sides)):
            if G == 1:
                hist = jnp.where(first, h_ref[0, :, cs], carry_scr[j, s, :, cs])
                tail = u[tm - BUF_ROWS:, :]
                carry_scr[j, s, :, cs] = tail
                t_ref[0, :, cs] = tail
                ext_scr[s, 0, 0:BUF_ROWS, cs] = hist
            else:
                t_ref[:, :, cs] = u.reshape(G, Tt, sub)
                ext_scr[s, :, 0:BUF_ROWS, cs] = h_ref[:, :, cs]
            ext_scr[s, :, BUF_ROWS:BUF_ROWS + Tt, cs] = u.reshape(G, Tt, sub)
            y = None
            for t in range(FFN_CONV):
                off = BUF_ROWS - (FFN_CONV - 1) + t
                term = ext_scr[s, :, off:off + Tt, cs] * cw_ref[t:t + 1, cs][None]
                y = term if y is None else y + term
            ys.append(y + cb_ref[:, cs][None])
        yg, yv = ys
        h = (yg * _sigmoid(yg) * yv).reshape(tm, sub).astype(BF16)
        p = _dot(h, wd_ref[cs, :])
        part = p if part is None else part + p
    acc_scr[...] += part

    @pl.when(j == nj - 1)
    def _():
        hsum = DEEPNORM_ALPHA * x_ref[...] + acc_scr[...]
        hc = hsum - jnp.mean(hsum, axis=-1, keepdims=True)
        var = jnp.mean(hc * hc, axis=-1, keepdims=True)
        y_ref[...] = hc * lax.rsqrt(var + LN_EPS) * g_ref[...] + b_ref[...]


def _ffn(x, w_up, conv_w, conv_b, hist, w_down, g, b, *, B, T, tm, tf):
    M, D = x.shape
    ffp = w_down.shape[0]
    nj = ffp // tf
    long_seq = T >= tm
    G, Tt = (1, tm) if long_seq else (tm // T, T)
    assert M % tm == 0 and T % Tt == 0 and (long_seq or Tt == BUF_ROWS)
    tiles_per_seq = T // Tt if long_seq else 1
    seq = (lambda i: i // tiles_per_seq) if long_seq else (lambda i: i)
    colspec = lambda rows: pl.BlockSpec((rows, tf), lambda i, j: (0, j))
    hspec = pl.BlockSpec((G, BUF_ROWS, tf), lambda i, j: (seq(i), 0, j))
    tspec = pl.BlockSpec((G, BUF_ROWS, tf), lambda i, j: (i, 0, j))
    tshape = jax.ShapeDtypeStruct((M // tm * G, BUF_ROWS, ffp), F32)
    y, tail_g, tail_v = pl.pallas_call(
        functools.partial(_ffn_kernel, G=G, Tt=Tt, nj=nj, tiles_per_seq=tiles_per_seq),
        grid=(M // tm, nj),
        in_specs=[pl.BlockSpec((tm, D), lambda i, j: (i, 0)),
                  colspec(D), colspec(D),
                  colspec(FFN_CONV), colspec(FFN_CONV),
                  colspec(1), colspec(1),
                  hspec, hspec,
                  pl.BlockSpec((tf, D), lambda i, j: (j, 0)),
                  pl.BlockSpec((1, D), lambda i, j: (0, 0)),
                  pl.BlockSpec((1, D), lambda i, j: (0, 0))],
        out_specs=[pl.BlockSpec((tm, D), lambda i, j: (i, 0)), tspec, tspec],
        out_shape=[jax.ShapeDtypeStruct((M, D), F32), tshape, tshape],
        scratch_shapes=[pltpu.VMEM((tm, D), BF16),
                        pltpu.VMEM((tm, D), F32),
                        pltpu.VMEM((2, G, BUF_ROWS + Tt, tf), F32),
                        pltpu.VMEM((nj, 2, BUF_ROWS, tf), F32)],
        compiler_params=_cparams("arbitrary", "arbitrary"),
        name="conv_ffn",
    )(x, w_up[0], w_up[1], conv_w[0], conv_w[1], conv_b[0].reshape(1, ffp), conv_b[1].reshape(1, ffp),
      hist[0], hist[1], w_down, g.reshape(1, D), b.reshape(1, D))
    last = slice(tiles_per_seq - 1, None, tiles_per_seq)
    return y, tail_g[last], tail_v[last]


def _lane_col(x, lane_iota, idx):
    return jnp.sum(jnp.where(lane_iota == idx, x, 0.0), axis=1, keepdims=True)


def _delta_kernel(q_ref, k_ref, v_ref, gate_ref, ab_ref, s0_ref, alog_ref, dtb_ref, nw_ref,
                  o_ref, sout_ref, s_scr, *, C, nchunks):
    n = pl.program_id(1)

    @pl.when(n == 0)
    def _():
        s_scr[...] = s0_ref[0]

    H = DN_HEADS
    gh = min(H, LANES // C)
    R = gh * C
    groups = [list(range(g * gh, (g + 1) * gh)) for g in range(H // gh)]
    hsl = lambda h: slice(h * HEAD_DIM, (h + 1) * HEAD_DIM)
    rsl = lambda j: slice(j * C, (j + 1) * C)
    cat0 = lambda xs: xs[0] if len(xs) == 1 else jnp.concatenate(xs, axis=0)

    row = lax.broadcasted_iota(jnp.int32, (R, R), 0)
    col = lax.broadcasted_iota(jnp.int32, (R, R), 1)
    same = (row // C) == (col // C)
    eye = row == col
    incl = same & (row >= col)
    strict = same & (row > col)
    eyef = jnp.where(eye, 1.0, 0.0).astype(F32)
    lane = lax.broadcasted_iota(jnp.int32, (C, LANES), 1)
    rowhead = lax.broadcasted_iota(jnp.int32, (R, HEAD_DIM), 0) // C

    ab = ab_ref[...]
    z = ab + dtb_ref[...]
    softplus = jnp.maximum(z, 0.0) + jnp.log1p(jnp.exp(-jnp.abs(z)))
    g_all = -jnp.exp(alog_ref[...]) * softplus
    beta_all = _sigmoid(ab)
    trow = lax.broadcasted_iota(jnp.int32, (C, C), 0)
    tcol = lax.broadcasted_iota(jnp.int32, (C, C), 1)
    tril = jnp.where(trow >= tcol, 1.0, 0.0).astype(F32)
    if C >= 2 * SUBLANES:
        gc_all = _dot3(tril, g_all)
    else:
        gc_all = jnp.dot(tril, g_all, preferred_element_type=F32, precision=lax.Precision.HIGHEST)

    q = [cat0([q_ref[:, hsl(h)] for h in hs]) for hs in groups]
    k = [cat0([k_ref[:, hsl(h)] for h in hs]) for hs in groups]
    v = [cat0([v_ref[:, hsl(h)] for h in hs]) for hs in groups]
    gcol_h = [_lane_col(gc_all, lane, h) for h in range(H)]
    gcol = [cat0([gcol_h[h] for h in hs]) for hs in groups]
    bcol = [cat0([_lane_col(beta_all, lane, H + h) for h in hs]) for hs in groups]
    glast_h = [g[C - 1:C, :] for g in gcol_h]
    glast = [cat0([jnp.broadcast_to(glast_h[h], (C, 1)) for h in hs]) for hs in groups]
    grow = [jnp.sum(jnp.where(eye, g, 0.0), axis=0, keepdims=True) for g in gcol]
    decay = [jnp.where(incl, jnp.exp(jnp.where(incl, gc - gr, 0.0)), 0.0) for gc, gr in zip(gcol, grow)]
    kb = [ki * bi for ki, bi in zip(k, bcol)]
    kq = [_dot_nt(jnp.concatenate([kbi, qi], axis=0).astype(BF16), ki.astype(BF16))
          for kbi, qi, ki in zip(kb, q, k)]
    m = [jnp.where(strict, x[:R] * d, 0.0) for x, d in zip(kq, decay)]
    attn = [x[R:] * d for x, d in zip(kq, decay)]

    nm = [-x for x in m]
    p = [eyef + x for x in nm]
    nk = [_dot3(x, x) for x in nm]
    terms = 2
    while terms * 2 < C:
        res = [_dot3(jnp.concatenate([pi, ni], axis=0), ni) for pi, ni in zip(p, nk)]
        p = [pi + r[:R] for pi, r in zip(p, res)]
        nk = [r[R:] for r in res]
        terms *= 2
    p = [pi + _dot3(pi, ni) for pi, ni in zip(p, nk)]

    eg = [jnp.exp(g) for g in gcol]
    uw = [_dot(pi.astype(BF16), jnp.concatenate([vi * bi, kbi * e], axis=1).astype(BF16))
          for pi, vi, bi, kbi, e in zip(p, v, bcol, kb, eg)]
    qg = [qi * e for qi, e in zip(q, eg)]
    kg = [ki * jnp.exp(gl - gc) for ki, gl, gc in zip(k, glast, gcol)]

    s_old = [s_scr[h] for h in range(H)]
    sb = [s.astype(BF16) for s in s_old]
    ws = []
    for g, hs in enumerate(groups):
        for j, h in enumerate(hs):
            lhs = jnp.concatenate([uw[g][rsl(j), HEAD_DIM:], qg[g][rsl(j), :]], axis=0).astype(BF16)
            ws.append(_dot(lhs, sb[h]))
    v_new = [cat0([uw[g][rsl(j), :HEAD_DIM] - ws[h][:C] for j, h in enumerate(hs)])
             for g, hs in enumerate(groups)]
    vnb = [x.astype(BF16) for x in v_new]
    o_all = [cat0([ws[h][C:] for h in hs]) + _dot(a.astype(BF16), vb)
             for hs, a, vb in zip(groups, attn, vnb)]
    for g, hs in enumerate(groups):
        for j, h in enumerate(hs):
            kgm = jnp.where(rowhead == j, kg[g], 0.0).astype(BF16) if gh > 1 else kg[g].astype(BF16)
            s_scr[h] = s_old[h] * jnp.exp(glast_h[h]) + _dot_tn(kgm, vnb[g])
    for g, hs in enumerate(groups):
        for j, h in enumerate(hs):
            o = o_all[g][rsl(j), :]
            o = o * lax.rsqrt(jnp.mean(o * o, axis=-1, keepdims=True) + NORM_EPS) * nw_ref[...]
            gate = gate_ref[:, hsl(h)]
            o_ref[:, hsl(h)] = (o * (gate * _sigmoid(gate))).astype(o_ref.dtype)

    @pl.when(n == nchunks - 1)
    def _():
        sout_ref[0] = s_scr[...]


def _delta(qkvn, proj, ab, s0, a_log, dt_bias, norm_w, *, B, T, C):
    nchunks = T // C
    W = DN_HEADS * HEAD_DIM
    pad = LANES - DN_HEADS
    alog_p = jnp.pad(a_log.astype(F32), (0, pad)).reshape(1, LANES)
    dtb_p = jnp.pad(dt_bias.astype(F32), (0, pad)).reshape(1, LANES)
    rows = lambda b, n: b * nchunks + n
    return pl.pallas_call(
        functools.partial(_delta_kernel, C=C, nchunks=nchunks),
        grid=(B, nchunks),
        in_specs=[pl.BlockSpec((C, W), lambda b, n: (rows(b, n), 0)),
                  pl.BlockSpec((C, W), lambda b, n: (rows(b, n), 1)),
                  pl.BlockSpec((C, W), lambda b, n: (rows(b, n), 2)),
                  pl.BlockSpec((C, W), lambda b, n: (rows(b, n), 3)),
                  pl.BlockSpec((C, LANES), lambda b, n: (rows(b, n), 0)),
                  pl.BlockSpec((1, DN_HEADS, HEAD_DIM, HEAD_DIM), lambda b, n: (b, 0, 0, 0)),
                  pl.BlockSpec((1, LANES), lambda b, n: (0, 0)),
                  pl.BlockSpec((1, LANES), lambda b, n: (0, 0)),
                  pl.BlockSpec((1, HEAD_DIM), lambda b, n: (0, 0))],
        out_specs=[pl.BlockSpec((C, W), lambda b, n: (rows(b, n), 0)),
                   pl.BlockSpec((1, DN_HEADS, HEAD_DIM, HEAD_DIM), lambda b, n: (b, 0, 0, 0))],
        out_shape=[jax.ShapeDtypeStruct((B * T, W), BF16),
                   jax.ShapeDtypeStruct((B, DN_HEADS, HEAD_DIM, HEAD_DIM), F32)],
        scratch_shapes=[pltpu.VMEM((DN_HEADS, HEAD_DIM, HEAD_DIM), F32)],
        compiler_params=_cparams("parallel", "arbitrary"),
        name="delta_rule",
    )(qkvn, qkvn, qkvn, proj, ab, s0, alog_p, dtb_p, norm_w.reshape(1, HEAD_DIM))


def _heads_to_lanes(ref, nheads, ntok):
    return jnp.concatenate(
        [ref[pl.ds(h, ntok, stride=nheads), :].astype(BF16) for h in range(nheads)], axis=1)


def _sb_blocks(qb, kblks, vblks, bias, carry, masks, ustrict):
    zs = [_dot_nt(qb, kb) * ATT_SCALE + bias for kb in kblks]
    lks = [jnp.minimum(-z, 0.0) - jnp.log(1.0 + jnp.exp(-jnp.abs(z))) for z in zs]
    lss = [z + lk for z, lk in zip(zs, lks)]
    lks = [lk if mk is None else jnp.where(mk, lk, 0.0) for lk, mk in zip(lks, masks)]
    parts = [_split2(lk) for lk in lks]
    rs = [_dot(hi, ustrict) + _dot(lo, ustrict) for hi, lo in parts]
    sums = [jnp.sum(lk, axis=1, keepdims=True) for lk in lks]
    acc = None
    for ls, r, sm, mk, vb in zip(lss, rs, sums, masks, vblks):
        a = jnp.exp(ls + r + carry)
        if mk is not None:
            a = jnp.where(mk, a, 0.0)
        pv = _dot(a.astype(BF16), vb)
        acc = pv if acc is None else acc + pv
        carry = carry + sm
    return acc, carry


def _sbp_kernel(bias_ref, q_ref, k_ref, v_ref, o_ref, kb_scr, vb_scr, *, tq):
    h = pl.program_id(1)
    qi = pl.program_id(2)

    @pl.when(qi == 0)
    def _():
        kb_scr[...] = k_ref[...].astype(BF16)
        vb_scr[...] = v_ref[...].astype(BF16)

    qb = q_ref[...].astype(BF16)
    bias = bias_ref[h]
    row = lax.broadcasted_iota(jnp.int32, (tq, tq), 0)
    col = lax.broadcasted_iota(jnp.int32, (tq, tq), 1)
    ustrict = jnp.where(row > col, 1.0, 0.0).astype(BF16)

    def blocks(kjs, carry, masks):
        starts = [kj * tq if isinstance(kj, int) else pl.multiple_of(kj * tq, tq) for kj in kjs]
        return _sb_blocks(qb, [kb_scr[pl.ds(s, tq), :] for s in starts],
                          [vb_scr[pl.ds(s, tq), :] for s in starts], bias, carry, masks, ustrict)

    acc, carry = blocks([qi], jnp.zeros((tq, 1), F32), [col < row])

    def group(n, first_kj):
        def run(st):
            acc, carry = st
            pv, carry = blocks([first_kj - d for d in range(n)], carry, [None] * n)
            return acc + pv, carry
        return run

    acc, carry = lax.fori_loop(0, qi // 4, lambda it, st: group(4, qi - 1 - 4 * it)(st), (acc, carry))
    rest = qi % 4
    acc, carry = lax.cond(rest >= 2, group(2, rest - 1), lambda st: st, (acc, carry))
    acc, _ = lax.cond(rest % 2 == 1, group(1, 0), lambda st: st, (acc, carry))
    o_ref[...] = acc.astype(o_ref.dtype)


def _sb_prompt(proj, sb_bias, *, B, T, qcol, kcol, vcol, tq):
    nq = T // tq
    grid_spec = pltpu.PrefetchScalarGridSpec(
        num_scalar_prefetch=1,
        grid=(B, SB_HEADS, nq),
        in_specs=[pl.BlockSpec((tq, HEAD_DIM), lambda b, h, i, s: (b * nq + i, qcol + h)),
                  pl.BlockSpec((T, HEAD_DIM), lambda b, h, i, s: (b, kcol + h)),
                  pl.BlockSpec((T, HEAD_DIM), lambda b, h, i, s: (b, vcol + h))],
        out_specs=pl.BlockSpec((tq, HEAD_DIM), lambda b, h, i, s: (b * nq + i, h)),
        scratch_shapes=[pltpu.VMEM((T, HEAD_DIM), BF16), pltpu.VMEM((T, HEAD_DIM), BF16)],
    )
    return pl.pallas_call(
        functools.partial(_sbp_kernel, tq=tq),
        grid_spec=grid_spec,
        out_shape=jax.ShapeDtypeStruct((B * T, SB_HEADS * HEAD_DIM), BF16),
        compiler_params=_cparams("parallel", "parallel", "arbitrary"),
        name="sb_prompt",
    )(sb_bias.astype(F32), proj, proj, proj)


def _sbs_kernel(pt_ref, q_ref, kn_ref, vn_ref, *rest, T, nsteps, pps):
    kc_refs, vc_refs = rest[:pps], rest[pps:2 * pps]
    bias_ref, o_ref, qbd_scr, knp_scr, vnp_scr, acc_scr, carry_scr = rest[2 * pps:]
    p = pl.program_id(1)
    R = SB_HEADS * T
    W = SB_HEADS * HEAD_DIM
    rrow = lax.broadcasted_iota(jnp.int32, (R, W), 0)
    rcol = lax.broadcasted_iota(jnp.int32, (R, W), 1)
    own_head = (rrow // T) == (rcol // HEAD_DIM)
    row = lax.broadcasted_iota(jnp.int32, (PAGE_SIZE, PAGE_SIZE), 0)
    col = lax.broadcasted_iota(jnp.int32, (PAGE_SIZE, PAGE_SIZE), 1)
    ustrict = jnp.where(row > col, 1.0, 0.0).astype(BF16)
    bias = bias_ref[...]

    def step(kblks, vblks, masks):
        pv, carry = _sb_blocks(qbd_scr[...], kblks, vblks, bias, carry_scr[...], masks, ustrict)
        acc_scr[...] += pv
        carry_scr[...] = carry

    @pl.when(p == 0)
    def _():
        q = q_ref[...]
        qrep = jnp.concatenate([q] * SB_HEADS, axis=0)
        qbd_scr[...] = jnp.where(own_head, qrep, 0.0).astype(BF16)
        knp_scr[...] = jnp.zeros_like(knp_scr)
        vnp_scr[...] = jnp.zeros_like(vnp_scr)
        knp_scr[0:T, :] = kn_ref[...]
        vnp_scr[0:T, :] = vn_ref[...]
        acc_scr[...] = jnp.zeros_like(acc_scr)
        carry_scr[...] = jnp.zeros_like(carry_scr)
        qpos = lax.broadcasted_iota(jnp.int32, (R, PAGE_SIZE), 0) % T
        kpos = lax.broadcasted_iota(jnp.int32, (R, PAGE_SIZE), 1)
        step([knp_scr[...].astype(BF16)], [vnp_scr[...].astype(BF16)], [kpos < qpos])

    step([_heads_to_lanes(r.at[0], SB_HEADS, PAGE_SIZE) for r in kc_refs],
         [_heads_to_lanes(r.at[0], SB_HEADS, PAGE_SIZE) for r in vc_refs], [None] * pps)

    @pl.when(p == nsteps - 1)
    def _():
        accm = jnp.where(own_head, acc_scr[...], 0.0)
        out = accm[0:T]
        for h in range(1, SB_HEADS):
            out = out + accm[h * T:(h + 1) * T]
        o_ref[...] = out.astype(o_ref.dtype)


def _sb_sample(proj, cache_k, cache_v, page_table, sb_bias, *, B, T, qcol, kcol, vcol):
    npages = page_table.shape[1]
    pps = SB_PAGES_PER_STEP if npages % SB_PAGES_PER_STEP == 0 else 1
    nsteps = npages // pps
    W = SB_HEADS * HEAD_DIM
    R = SB_HEADS * T
    bias_rows = jnp.broadcast_to(jnp.repeat(sb_bias.astype(F32), T)[:, None], (R, PAGE_SIZE))

    def page_spec(i):
        return pl.BlockSpec((1, PAGE_SIZE * SB_HEADS, HEAD_DIM),
                            lambda b, p, pt: (pt[b, npages - 1 - (p * pps + i)], 0, 0))

    grid_spec = pltpu.PrefetchScalarGridSpec(
        num_scalar_prefetch=1,
        grid=(B, nsteps),
        in_specs=[pl.BlockSpec((T, W), lambda b, p, pt: (b, qcol)),
                  pl.BlockSpec((T, W), lambda b, p, pt: (b, kcol)),
                  pl.BlockSpec((T, W), lambda b, p, pt: (b, vcol))]
                 + [page_spec(i) for i in range(pps)] * 2
                 + [pl.BlockSpec((R, PAGE_SIZE), lambda b, p, pt: (0, 0))],
        out_specs=pl.BlockSpec((T, W), lambda b, p, pt: (b, 0)),
        scratch_shapes=[pltpu.VMEM((R, W), BF16),
                        pltpu.VMEM((PAGE_SIZE, W), F32),
                        pltpu.VMEM((PAGE_SIZE, W), F32),
                        pltpu.VMEM((R, W), F32),
                        pltpu.VMEM((R, 1), F32)],
    )
    return pl.pallas_call(
        functools.partial(_sbs_kernel, T=T, nsteps=nsteps, pps=pps),
        grid_spec=grid_spec,
        out_shape=jax.ShapeDtypeStruct((B * T, W), BF16),
        compiler_params=_cparams("arbitrary", "arbitrary"),
        name="sb_sample",
    )(page_table, proj, proj, proj, *([cache_k] * pps), *([cache_v] * pps), bias_rows)


def _xattn_prompt_kernel(q_ref, k_ref, v_ref, o_ref):
    for h in range(X_HEADS):
        sl = slice(h * HEAD_DIM, (h + 1) * HEAD_DIM)
        s = _dot_nt(q_ref[:, sl].astype(BF16), k_ref[0, :, sl].astype(BF16)) * ATT_SCALE
        p = jnp.exp(s - jnp.max(s, axis=-1, keepdims=True))
        p = p / jnp.sum(p, axis=-1, keepdims=True)
        o_ref[:, sl] = _dot(p.astype(BF16), v_ref[0, :, sl].astype(BF16)).astype(o_ref.dtype)


def _xattn_prompt(q, mk, mv, *, B, T, tq):
    W = X_HEADS * HEAD_DIM
    nq = T // tq
    nm = mk.shape[1]
    return pl.pallas_call(
        _xattn_prompt_kernel,
        grid=(B, nq),
        in_specs=[pl.BlockSpec((tq, W), lambda b, i: (b * nq + i, 0)),
                  pl.BlockSpec((1, nm, W), lambda b, i: (b, 0, 0)),
                  pl.BlockSpec((1, nm, W), lambda b, i: (b, 0, 0))],
        out_specs=pl.BlockSpec((tq, W), lambda b, i: (b * nq + i, 0)),
        out_shape=jax.ShapeDtypeStruct((B * T, W), BF16),
        compiler_params=_cparams("parallel", "parallel"),
        name="xattn_prompt",
    )(q, mk, mv)


def _xattn_sample_kernel(q_ref, k_ref, v_ref, o_ref, *, G, T, nm):
    R = X_HEADS * T
    W = X_HEADS * HEAD_DIM
    rrow = lax.broadcasted_iota(jnp.int32, (R, W), 0)
    rcol = lax.broadcasted_iota(jnp.int32, (R, W), 1)
    own_head = (rrow // T) == (rcol // HEAD_DIM)
    for g in range(G):
        q = q_ref[g]
        qbd = jnp.where(own_head, jnp.concatenate([q] * X_HEADS, axis=0), 0.0).astype(BF16)
        s = _dot_nt(qbd, _heads_to_lanes(k_ref.at[g], X_HEADS, nm)) * ATT_SCALE
        p = jnp.exp(s - jnp.max(s, axis=-1, keepdims=True))
        p = p / jnp.sum(p, axis=-1, keepdims=True)
        of = jnp.where(own_head, _dot(p.astype(BF16), _heads_to_lanes(v_ref.at[g], X_HEADS, nm)), 0.0)
        out = of[0:T]
        for h in range(1, X_HEADS):
            out = out + of[h * T:(h + 1) * T]
        o_ref[g] = out.astype(o_ref.dtype)


def _xattn_sample(q3, mk, mv, layer, *, G):
    B, T, W = q3.shape
    nm = mk.shape[2] // X_HEADS
    return pl.pallas_call(
        functools.partial(_xattn_sample_kernel, G=G, T=T, nm=nm),
        grid=(B // G,),
        in_specs=[pl.BlockSpec((G, T, W), lambda b: (b, 0, 0)),
                  pl.BlockSpec((None, G, nm * X_HEADS, HEAD_DIM), lambda b: (layer, b, 0, 0)),
                  pl.BlockSpec((None, G, nm * X_HEADS, HEAD_DIM), lambda b: (layer, b, 0, 0))],
        out_specs=pl.BlockSpec((G, T, W), lambda b: (b, 0, 0)),
        out_shape=jax.ShapeDtypeStruct((B, T, W), BF16),
        compiler_params=_cparams("parallel"),
        name="xattn_sample",
    )(q3, mk, mv)


def _lru_kernel(y_ref, xc_ref, wr_ref, wi_ref, br_ref, bi_ref, lam_ref, h0_ref, o_ref, hl_ref,
                hc_scr, a_scr, b_scr, h_scr, *, G, Tt, nt):
    t = pl.program_id(1)
    rows = G * Tt
    ngroups = rows // SUBLANES
    tc = a_scr.shape[1]
    wb = tc // LRU_BLOCKS

    @pl.when(t == 0)
    def _():
        hc_scr[...] = h0_ref[...]

    xc = xc_ref[...].reshape(rows, tc)
    xbs = [xc[:, n * wb:(n + 1) * wb].astype(BF16) for n in range(LRU_BLOCKS)]
    gate = lambda w_ref, b_ref: _sigmoid(
        jnp.concatenate([_dot(xbs[n], w_ref[n]) for n in range(LRU_BLOCKS)], axis=1) + b_ref[...])
    r = gate(wr_ref, br_ref)
    i = gate(wi_ref, bi_ref)
    log_a = LRU_C * r * _neg_softplus(-lam_ref[...])
    a = jnp.exp(log_a)
    a_scr[...] = a
    b_scr[...] = jnp.sqrt(-jnp.tanh(log_a) * (a * a + 1.0)) * (i * xc)
    sub = lax.broadcasted_iota(jnp.int32, (SUBLANES, tc), 0)

    def group(gi, carry):
        r0 = pl.multiple_of(gi * SUBLANES, SUBLANES)
        a = a_scr[pl.ds(r0, SUBLANES), :]
        b = b_scr[pl.ds(r0, SUBLANES), :]
        for s in (1, 2, 4):
            keep = sub >= s
            a_sh = jnp.where(keep, pltpu.roll(a, s, 0), 1.0)
            b_sh = jnp.where(keep, pltpu.roll(b, s, 0), 0.0)
            b = a * b_sh + b
            a = a * a_sh
        hprev = hc_scr[gi] if Tt == SUBLANES else carry
        h8 = b + a * hprev
        h_scr[pl.ds(r0, SUBLANES), :] = h8
        last = h8[SUBLANES - 1:SUBLANES, :]
        if Tt == SUBLANES:
            hl_ref[gi] = last
            return carry
        return last

    carry = lax.fori_loop(0, ngroups, group, hc_scr[0])
    if Tt != SUBLANES:
        hc_scr[0] = carry

        @pl.when(t == nt - 1)
        def _():
            hl_ref[0] = carry

    y = y_ref[...].reshape(rows, tc)
    o_ref[...] = (jax.nn.gelu(y) * h_scr[...]).astype(o_ref.dtype).reshape(G, Tt, tc)


def _lru(proj3, xc3, w_r, w_i, b_r, b_i, lam, h0, *, G, Tt):
    B, T, D = xc3.shape
    wb = D // LRU_BLOCKS
    nt = T // Tt
    assert Tt % SUBLANES == 0 and (G == 1 or Tt == SUBLANES)
    vec = lambda a: a.astype(F32).reshape(1, D)
    whole = lambda shape: pl.BlockSpec(shape, lambda b, t: (0,) * len(shape))
    return pl.pallas_call(
        functools.partial(_lru_kernel, G=G, Tt=Tt, nt=nt),
        grid=(B // G, nt),
        in_specs=[pl.BlockSpec((G, Tt, D), lambda b, t: (b, t, 0)),
                  pl.BlockSpec((G, Tt, D), lambda b, t: (b, t, 0)),
                  whole((LRU_BLOCKS, wb, wb)), whole((LRU_BLOCKS, wb, wb)),
                  whole((1, D)), whole((1, D)), whole((1, D)),
                  pl.BlockSpec((G, 1, D), lambda b, t: (b, 0, 0))],
        out_specs=[pl.BlockSpec((G, Tt, D), lambda b, t: (b, t, 0)),
                   pl.BlockSpec((G, 1, D), lambda b, t: (b, 0, 0))],
        out_shape=[jax.ShapeDtypeStruct((B, T, D), BF16),
                   jax.ShapeDtypeStruct((B, 1, D), F32)],
        scratch_shapes=[pltpu.VMEM((G, 1, D), F32),
                        pltpu.VMEM((G * Tt, D), F32),
                        pltpu.VMEM((G * Tt, D), F32),
                        pltpu.VMEM((G * Tt, D), F32)],
        compiler_params=_cparams("parallel", "arbitrary"),
        name="rglru",
    )(proj3, xc3, w_r, w_i, vec(b_r), vec(b_i), vec(lam), h0.reshape(B, 1, D))


def _pad_buf(buf):
    return jnp.pad(buf, ((0, 0), (BUF_ROWS - buf.shape[1], 0), (0, 0)))


def _row_tile(m):
    return 512 if m % 512 == 0 else m


def _col_tile(n):
    return 1024 if n % 1024 == 0 else 512 if n % 512 == 0 else n


def _wblocks(w):
    return w.astype(BF16)


class _Stream:
    def __init__(self, B, T):
        self.B, self.T = B, T
        self.M = B * T
        self.long = T > SUBLANES
        self.G = 1 if self.long else B


def _ab_layer(st, x, w, dn_conv, dn_S, sb_fn):
    B, T, M = st.B, st.T, st.M
    tm = _row_tile(M)
    proj = _mm(x, w["ab_main"], tm=tm)
    ab = _mm(x, w["ab_ab"], tm=tm)
    proj3 = proj.reshape(B, T, -1)
    nqkv = 3 * DN_HEADS * HEAD_DIM
    qkvn = _conv_dn(proj3, _pad_buf(dn_conv), w["ab_conv_w"], G=st.G)
    C = 64 if st.long else T
    o_dn, s_new = _delta(qkvn.reshape(M, nqkv), proj, ab, dn_S, w["dn_A_log"], w["dn_dt_bias"],
                         w["dn_norm_w"], B=B, T=T, C=C)
    o_sb = sb_fn(proj)
    o = jnp.concatenate([o_dn, o_sb], axis=-1)
    new_conv = proj3[:, T - (DN_CONV - 1):, :nqkv]
    W = SB_HEADS * HEAD_DIM
    k_new = proj[:, 5 * W:6 * W].reshape(B, T, SB_HEADS, HEAD_DIM)
    v_new = proj[:, 6 * W:7 * W].reshape(B, T, SB_HEADS, HEAD_DIM)
    return o, w["ab_w_out"], (k_new, v_new, new_conv, s_new)


def _lru_layer(st, x, w, lru_conv, lru_h):
    B, T, M = st.B, st.T, st.M
    D = x.shape[1]
    proj = _mm(x, w["c_w_in"], tm=_row_tile(M))
    proj3 = proj.reshape(B, T, 2 * D)
    tc = 256
    xc3 = _conv_lin(proj3, D // tc, _pad_buf(lru_conv), w["lru_conv_w"], w["lru_conv_b"], G=st.G, tc=tc)
    G, Tt = (1, 256) if st.long else (min(32, B), T)
    o3, h_last = _lru(proj3, xc3, w["lru_w_r"], w["lru_w_i"], w["lru_b_r"], w["lru_b_i"],
                      w["lru_lambda"], lru_h, G=G, Tt=Tt)
    new_conv = proj3[:, T - (LRU_CONV - 1):, D:]
    return o3.reshape(M, D), w["c_w_out"], (new_conv, h_last.reshape(B, D))


def _xattn_layer(st, x, w, mk3, mv3, layer):
    B, T, M = st.B, st.T, st.M
    q = _mm(x, w["xa_w_q"], tm=_row_tile(M))
    if st.long:
        return _xattn_prompt(q, mk3, mv3, B=B, T=T, tq=512)
    W = q.shape[1]
    return _xattn_sample(q.reshape(B, T, W), mk3, mv3, layer, G=8).reshape(M, W)


def _ffn_layer(st, x, w, ffn_conv, d_ff, g, b):
    B, T, M = st.B, st.T, st.M
    ffp = w["ffn_w_down"].shape[0]
    halves = lambda a: (a[..., :d_ff], a[..., d_ff:])
    padc = lambda a: jnp.pad(a, [(0, 0)] * (a.ndim - 1) + [(0, ffp - d_ff)])
    hist = tuple(_pad_buf(padc(h)) for h in halves(ffn_conv))
    y, tail_g, tail_v = _ffn(x, w["ffn_w_up"], w["ffn_conv_w"], w["ffn_conv_b"], hist,
                             w["ffn_w_down"], g, b, B=B, T=T, tm=_row_tile(M), tf=FFN_COL_TILE)
    keep = slice(BUF_ROWS - (FFN_CONV - 1), BUF_ROWS)
    new_conv = jnp.concatenate([tail_g[:, keep, :d_ff], tail_v[:, keep, :d_ff]], axis=-1)
    return y, new_conv


def kernel(x_prompt, x_sample, mem_prompt, cache_sb_k, cache_sb_v, cache_mem_k, cache_mem_v, state_dn_conv, state_dn_S, state_lru_conv, state_lru_h, state_ffn_conv, page_table, ab_w_in, ab_conv_w, dn_A_log, dn_dt_bias, dn_norm_w, sb_bias, ab_w_out, c_w_in, lru_conv_w, lru_conv_b, lru_w_r, lru_b_r, lru_w_i, lru_b_i, lru_lambda, c_w_out, xa_w_q, xa_w_k, xa_w_v, xa_w_o, ffn_w_up, ffn_conv_w, ffn_conv_b, ffn_w_down, ln_g, ln_b):
    bp, tp, d_model = x_prompt.shape
    bs, ts, _ = x_sample.shape
    sp, ss = _Stream(bp, tp), _Stream(bs, ts)
    dt = x_prompt.dtype
    n_mem = mem_prompt.shape[1]
    xw = X_HEADS * HEAD_DIM
    nqkv = 3 * DN_HEADS * HEAD_DIM
    d_ff = ffn_w_down.shape[1]
    ffp = -(-d_ff // FFN_COL_TILE) * FFN_COL_TILE
    depth = ln_g.shape[0]

    xp = x_prompt.reshape(sp.M, d_model)
    xs = x_sample.reshape(ss.M, d_model)
    mem2 = mem_prompt.reshape(bp * n_mem, d_model)

    outs_p = {k: [] for k in ("sb_k", "sb_v", "dn_conv", "dn_S", "lru_conv", "lru_h", "mem_k", "mem_v", "ffn_conv")}
    outs_s = {k: [] for k in ("sb_k", "sb_v", "dn_conv", "dn_S", "lru_conv", "lru_h", "ffn_conv")}

    def pad_ff(a, axis):
        pad = [(0, 0)] * a.ndim
        pad[axis] = (0, ffp - d_ff)
        return jnp.pad(a, pad)

    for layer in range(depth):
        w = {}
        if layer % 2 == 0:
            e = layer // 2
            w_in = ab_w_in[e]
            c_ab = nqkv + DN_HEADS * HEAD_DIM
            w["ab_main"] = _wblocks(jnp.concatenate([w_in[:, :c_ab], w_in[:, c_ab + 2 * DN_HEADS:]], axis=1))
            w["ab_ab"] = _wblocks(jnp.pad(w_in[:, c_ab:c_ab + 2 * DN_HEADS], ((0, 0), (0, LANES - 2 * DN_HEADS))))
            w["ab_conv_w"] = ab_conv_w[e]
            w["dn_A_log"], w["dn_dt_bias"], w["dn_norm_w"] = dn_A_log[e], dn_dt_bias[e], dn_norm_w[e]
            w["ab_w_out"] = ab_w_out[e].astype(BF16)
        else:
            o = layer // 2
            w["c_w_in"] = _wblocks(c_w_in[o])
            w["lru_conv_w"], w["lru_conv_b"] = lru_conv_w[o], lru_conv_b[o]
            w["lru_w_r"], w["lru_w_i"] = lru_w_r[o].astype(BF16), lru_w_i[o].astype(BF16)
            w["lru_b_r"], w["lru_b_i"], w["lru_lambda"] = lru_b_r[o], lru_b_i[o], lru_lambda[o]
            w["c_w_out"] = c_w_out[o].astype(BF16)
        w["xa_w_q"] = _wblocks(xa_w_q[layer])
        w["ffn_w_up"] = (pad_ff(ffn_w_up[layer][:, :d_ff].astype(BF16), 1),
                         pad_ff(ffn_w_up[layer][:, d_ff:].astype(BF16), 1))
        w["ffn_conv_w"] = (pad_ff(ffn_conv_w[layer][:, :d_ff], 1), pad_ff(ffn_conv_w[layer][:, d_ff:], 1))
        w["ffn_conv_b"] = (pad_ff(ffn_conv_b[layer][:d_ff], 0), pad_ff(ffn_conv_b[layer][d_ff:], 0))
        w["ffn_w_down"] = pad_ff(ffn_w_down[layer].astype(BF16), 0)
        w_xo = xa_w_o[layer].astype(BF16)

        if layer % 2 == 0:
            e = layer // 2
            qcol, kcol, vcol = 4 * DN_HEADS, 5 * DN_HEADS, 6 * DN_HEADS
            sbp = functools.partial(_sb_prompt, sb_bias=sb_bias[e], B=bp, T=tp, qcol=qcol, kcol=kcol, vcol=vcol, tq=256)
            op, wout, (k_new, v_new, cb, s_new) = _ab_layer(
                sp, xp, w, jnp.zeros((bp, DN_CONV - 1, nqkv), dt),
                jnp.zeros((bp, DN_HEADS, HEAD_DIM, HEAD_DIM), F32), sbp)
            outs_p["sb_k"].append(k_new); outs_p["sb_v"].append(v_new)
            outs_p["dn_conv"].append(cb); outs_p["dn_S"].append(s_new)
            n_pool = cache_sb_k.shape[1]
            ck = cache_sb_k[e].reshape(n_pool, PAGE_SIZE * SB_HEADS, HEAD_DIM)
            cv = cache_sb_v[e].reshape(n_pool, PAGE_SIZE * SB_HEADS, HEAD_DIM)
            sbs = functools.partial(_sb_sample, cache_k=ck, cache_v=cv, page_table=page_table, sb_bias=sb_bias[e],
                                    B=bs, T=ts, qcol=4, kcol=5, vcol=6)
            os_, _, (k_new, v_new, cb, s_new) = _ab_layer(ss, xs, w, state_dn_conv[e], state_dn_S[e], sbs)
            outs_s["sb_k"].append(k_new); outs_s["sb_v"].append(v_new)
            outs_s["dn_conv"].append(cb); outs_s["dn_S"].append(s_new)
        else:
            o = layer // 2
            op, wout, (cb, hl) = _lru_layer(sp, xp, w, jnp.zeros((bp, LRU_CONV - 1, d_model), dt),
                                            jnp.zeros((bp, d_model), dt))
            outs_p["lru_conv"].append(cb); outs_p["lru_h"].append(hl)
            os_, _, (cb, hl) = _lru_layer(ss, xs, w, state_lru_conv[o], state_lru_h[o])
            outs_s["lru_conv"].append(cb); outs_s["lru_h"].append(hl)
        xp = _mm_ln(op, wout, xp, ln_g[layer, 0], ln_b[layer, 0], tm=LN_ROW_TILE, tk=1024)
        xs = _mm_ln(os_, wout, xs, ln_g[layer, 0], ln_b[layer, 0], tm=LN_ROW_TILE, tk=1024)

        mk = _mm(mem2, _wblocks(xa_w_k[layer]), tm=_row_tile(bp * n_mem))
        mv = _mm(mem2, _wblocks(xa_w_v[layer]), tm=_row_tile(bp * n_mem))
        outs_p["mem_k"].append(mk.reshape(bp, n_mem, X_HEADS, HEAD_DIM))
        outs_p["mem_v"].append(mv.reshape(bp, n_mem, X_HEADS, HEAD_DIM))
        ap = _xattn_layer(sp, xp, w, mk.reshape(bp, n_mem, xw), mv.reshape(bp, n_mem, xw), layer)
        as_ = _xattn_layer(ss, xs, w, cache_mem_k.reshape(depth, bs, n_mem * X_HEADS, HEAD_DIM),
                           cache_mem_v.reshape(depth, bs, n_mem * X_HEADS, HEAD_DIM), layer)
        xp = _mm_ln(ap, w_xo, xp, ln_g[layer, 1], ln_b[layer, 1], tm=LN_ROW_TILE, tk=xw)
        xs = _mm_ln(as_, w_xo, xs, ln_g[layer, 1], ln_b[layer, 1], tm=LN_ROW_TILE, tk=xw)

        xp, fb = _ffn_layer(sp, xp, w, jnp.zeros((bp, FFN_CONV - 1, 2 * d_ff), dt), d_ff,
                            ln_g[layer, 2], ln_b[layer, 2])
        outs_p["ffn_conv"].append(fb)
        xs, fb = _ffn_layer(ss, xs, w, state_ffn_conv[layer], d_ff, ln_g[layer, 2], ln_b[layer, 2])
        outs_s["ffn_conv"].append(fb)

    st = lambda xs_: jnp.stack(xs_)
    return (xp.reshape(bp, tp, d_model), xs.reshape(bs, ts, d_model),
            st(outs_p["sb_k"]), st(outs_p["sb_v"]), st(outs_p["dn_conv"]), st(outs_p["dn_S"]),
            st(outs_p["lru_conv"]), st(outs_p["lru_h"]), st(outs_p["mem_k"]), st(outs_p["mem_v"]),
            st(outs_p["ffn_conv"]),
            st(outs_s["sb_k"]), st(outs_s["sb_v"]), st(outs_s["dn_conv"]), st(outs_s["dn_S"]),
            st(outs_s["lru_conv"]), st(outs_s["lru_h"]), st(outs_s["ffn_conv"]))
```

```python
import functools

import jax
import jax.numpy as jnp
from jax import lax
from jax.experimental import pallas as pl
from jax.experimental.pallas import tpu as pltpu

F32 = jnp.float32
BF16 = jnp.bfloat16

HEAD_DIM = 128
DN_HEADS = 8
SB_HEADS = 8
X_HEADS = 4
DN_CONV = 4
LRU_CONV = 4
FFN_CONV = 3
LRU_BLOCKS = 8
LRU_C = 8.0
PAGE_SIZE = 128
DEPTH = 2
DEEPNORM_ALPHA = (2.0 * DEPTH) ** 0.25
LN_EPS = 1e-5
NORM_EPS = 1e-6
ATT_SCALE = HEAD_DIM ** -0.5

LANES = 128
SUBLANES = 8
VMEM_LIMIT_BYTES = 56 * 1024 * 1024
BUF_ROWS = SUBLANES
SB_PAGES_PER_STEP = 8
LN_ROW_TILE = 512
FFN_COL_TILE = 512
FFN_SUB_TILE = 256


def _cparams(*sem):
    return pltpu.CompilerParams(dimension_semantics=sem, vmem_limit_bytes=VMEM_LIMIT_BYTES)


def _dot(a, b):
    return jnp.dot(a, b, preferred_element_type=F32)


def _dot_nt(a, b):
    return lax.dot_general(a, b, (((1,), (1,)), ((), ())), preferred_element_type=F32)


def _dot_tn(a, b):
    return lax.dot_general(a, b, (((0,), (0,)), ((), ())), preferred_element_type=F32)


def _split2(x):
    hi = x.astype(BF16)
    lo = (x - hi.astype(F32)).astype(BF16)
    return hi, lo


def _dot3(a, b):
    ah, al = _split2(a)
    bh, bl = _split2(b)
    return _dot(ah, bh) + _dot(ah, bl) + _dot(al, bh)


def _sigmoid(x):
    return 1.0 / (1.0 + jnp.exp(-x))


def _neg_softplus(z):
    return jnp.minimum(-z, 0.0) - jnp.log1p(jnp.exp(-jnp.abs(z)))


def _mm_kernel(x_ref, w_ref, o_ref, xb_ref):
    @pl.when(pl.program_id(1) == 0)
    def _():
        xb_ref[...] = x_ref[...].astype(BF16)

    o_ref[...] = _dot(xb_ref[...], w_ref[...]).astype(o_ref.dtype)


def _mm(x, w, *, tm, out_dtype=F32):
    M, K = x.shape
    N = w.shape[1]
    tn = _col_tile(N)
    assert M % tm == 0 and w.shape[0] == K
    return pl.pallas_call(
        _mm_kernel,
        grid=(M // tm, N // tn),
        in_specs=[pl.BlockSpec((tm, K), lambda i, j: (i, 0)),
                  pl.BlockSpec((K, tn), lambda i, j: (0, j))],
        out_specs=pl.BlockSpec((tm, tn), lambda i, j: (i, j)),
        out_shape=jax.ShapeDtypeStruct((M, N), out_dtype),
        scratch_shapes=[pltpu.VMEM((tm, K), BF16)],
        compiler_params=_cparams("parallel", "arbitrary"),
        name="proj",
    )(x, w)


def _mm_ln_kernel(x_ref, w_ref, r_ref, g_ref, b_ref, o_ref, acc_ref, *, nk):
    k = pl.program_id(1)
    part = _dot(x_ref[...].astype(BF16), w_ref[...])

    @pl.when(k == 0)
    def _():
        acc_ref[...] = part

    @pl.when(k > 0)
    def _():
        acc_ref[...] += part

    @pl.when(k == nk - 1)
    def _():
        h = DEEPNORM_ALPHA * r_ref[...] + acc_ref[...]
        hc = h - jnp.mean(h, axis=-1, keepdims=True)
        var = jnp.mean(hc * hc, axis=-1, keepdims=True)
        o_ref[...] = hc * lax.rsqrt(var + LN_EPS) * g_ref[...] + b_ref[...]


def _mm_ln(x, w, resid, g, b, *, tm, tk):
    M, K = x.shape
    N = w.shape[1]
    tm = min(tm, M)
    assert M % tm == 0 and K % tk == 0
    nk = K // tk
    return pl.pallas_call(
        functools.partial(_mm_ln_kernel, nk=nk),
        grid=(M // tm, nk),
        in_specs=[pl.BlockSpec((tm, tk), lambda i, k: (i, k)),
                  pl.BlockSpec((tk, N), lambda i, k: (k, 0)),
                  pl.BlockSpec((tm, N), lambda i, k: (i, 0)),
                  pl.BlockSpec((1, N), lambda i, k: (0, 0)),
                  pl.BlockSpec((1, N), lambda i, k: (0, 0))],
        out_specs=pl.BlockSpec((tm, N), lambda i, k: (i, 0)),
        out_shape=jax.ShapeDtypeStruct((M, N), F32),
        scratch_shapes=[pltpu.VMEM((tm, N), F32)],
        compiler_params=_cparams("parallel", "arbitrary"),
        name="proj_ln",
    )(x, w, resid, g.reshape(1, N), b.reshape(1, N))


def _conv_taps(u_ref, buf_ref, w_ref, ext_ref, width):
    T = u_ref.shape[1]
    ext_ref[:, 0:BUF_ROWS, :] = buf_ref[...]
    ext_ref[:, BUF_ROWS:BUF_ROWS + T, :] = u_ref[...]
    y = None
    for j in range(width):
        off = BUF_ROWS - (width - 1) + j
        term = ext_ref[:, off:off + T, :] * w_ref[j:j + 1, :][None]
        y = term if y is None else y + term
    return y


def _conv_dn_kernel(u_ref, buf_ref, w_ref, o_ref, ext_ref):
    c = pl.program_id(1)
    y = _conv_taps(u_ref, buf_ref, w_ref, ext_ref, DN_CONV)
    y = y * _sigmoid(y)
    ss = jnp.sum(y * y, axis=-1, keepdims=True)
    scale = jnp.where(c < DN_HEADS, ATT_SCALE, 1.0).astype(F32)
    yn = y * lax.rsqrt(ss + NORM_EPS) * scale
    o_ref[...] = jnp.where(c < 2 * DN_HEADS, yn, y)


def _conv_dn(u3, bufpad, w, *, G):
    B, T, _ = u3.shape
    C = w.shape[1]
    nc = C // HEAD_DIM
    return pl.pallas_call(
        _conv_dn_kernel,
        grid=(B // G, nc),
        in_specs=[pl.BlockSpec((G, T, HEAD_DIM), lambda b, c: (b, 0, c)),
                  pl.BlockSpec((G, BUF_ROWS, HEAD_DIM), lambda b, c: (b, 0, c)),
                  pl.BlockSpec((DN_CONV, HEAD_DIM), lambda b, c: (0, c))],
        out_specs=pl.BlockSpec((G, T, HEAD_DIM), lambda b, c: (b, 0, c)),
        out_shape=jax.ShapeDtypeStruct((B, T, C), F32),
        scratch_shapes=[pltpu.VMEM((G, BUF_ROWS + T, HEAD_DIM), F32)],
        compiler_params=_cparams("parallel", "parallel"),
        name="conv_dn",
    )(u3, bufpad, w)


def _conv_lin_kernel(u_ref, buf_ref, w_ref, b_ref, o_ref, ext_ref):
    y = _conv_taps(u_ref, buf_ref, w_ref, ext_ref, LRU_CONV)
    o_ref[...] = y + b_ref[...][None]


def _conv_lin(u3, col0, bufpad, w, bias, *, G, tc):
    B, T, _ = u3.shape
    C = w.shape[1]
    return pl.pallas_call(
        _conv_lin_kernel,
        grid=(B // G, C // tc),
        in_specs=[pl.BlockSpec((G, T, tc), lambda b, c: (b, 0, col0 + c)),
                  pl.BlockSpec((G, BUF_ROWS, tc), lambda b, c: (b, 0, c)),
                  pl.BlockSpec((LRU_CONV, tc), lambda b, c: (0, c)),
                  pl.BlockSpec((1, tc), lambda b, c: (0, c))],
        out_specs=pl.BlockSpec((G, T, tc), lambda b, c: (b, 0, c)),
        out_shape=jax.ShapeDtypeStruct((B, T, C), F32),
        scratch_shapes=[pltpu.VMEM((G, BUF_ROWS + T, tc), F32)],
        compiler_params=_cparams("parallel", "parallel"),
        name="conv_lin",
    )(u3, bufpad, w, bias.reshape(1, C))


def _ffn_kernel(x_ref, wg_ref, wv_ref, cwg_ref, cwv_ref, cbg_ref, cbv_ref, hg_ref, hv_ref, wd_ref,
                g_ref, b_ref, y_ref, tg_ref, tv_ref, xb_scr, acc_scr, ext_scr, carry_scr,
                *, G, Tt, nj, tiles_per_seq):
    i = pl.program_id(0)
    j = pl.program_id(1)
    tm, tf = G * Tt, ext_scr.shape[3]
    sub = min(tf, FFN_SUB_TILE)

    @pl.when(j == 0)
    def _():
        xb_scr[...] = x_ref[...].astype(BF16)
        acc_scr[...] = jnp.zeros_like(acc_scr)

    if G == 1:
        @pl.when((i == 0) & (j == 0))
        def _():
            carry_scr[...] = jnp.zeros_like(carry_scr)

    xb = xb_scr[...]
    sides = ((wg_ref, cwg_ref, cbg_ref, hg_ref, tg_ref), (wv_ref, cwv_ref, cbv_ref, hv_ref, tv_ref))
    subs = [slice(c, c + sub) for c in range(0, tf, sub)]
    ups = [[_dot(xb, side[0][:, cs]) for side in sides] for cs in subs]
    first = (i % tiles_per_seq) == 0
    part = None
    for cs, us in zip(subs, ups):
        ys = []
        for s, (u, (_, cw_ref, cb_ref, h_ref, t_ref)) in enumerate(zip(us, sides)):
            if G == 1:
                hist = jnp.where(first, h_ref[0, :, cs], carry_scr[j, s, :, cs])
                tail = u[tm - BUF_ROWS:, :]
                carry_scr[j, s, :, cs] = tail
                t_ref[0, :, cs] = tail
                ext_scr[s, 0, 0:BUF_ROWS, cs] = hist
            else:
                t_ref[:, :, cs] = u.reshape(G, Tt, sub)
                ext_scr[s, :, 0:BUF_ROWS, cs] = h_ref[:, :, cs]
            ext_scr[s, :, BUF_ROWS:BUF_ROWS + Tt, cs] = u.reshape(G, Tt, sub)
            y = None
            for t in range(FFN_CONV):
                off = BUF_ROWS - (FFN_CONV - 1) + t
                term = ext_scr[s, :, off:off + Tt, cs] * cw_ref[t:t + 1, cs][None]
                y = term if y is None else y + term
            ys.append(y + cb_ref[:, cs][None])
        yg, yv = ys
        h = (yg * _sigmoid(yg) * yv).reshape(tm, sub).astype(BF16)
        p = _dot(h, wd_ref[cs, :])
        part = p if part is None else part + p
    acc_scr[...] += part

    @pl.when(j == nj - 1)
    def _():
        hsum = DEEPNORM_ALPHA * x_ref[...] + acc_scr[...]
        hc = hsum - jnp.mean(hsum, axis=-1, keepdims=True)
        var = jnp.mean(hc * hc, axis=-1, keepdims=True)
        y_ref[...] = hc * lax.rsqrt(var + LN_EPS) * g_ref[...] + b_ref[...]


def _ffn(x, w_up, conv_w, conv_b, hist, w_down, g, b, *, B, T, tm, tf):
    M, D = x.shape
    ffp = w_down.shape[0]
    nj = ffp // tf
    long_seq = T >= tm
    G, Tt = (1, tm) if long_seq else (tm // T, T)
    assert M % tm == 0 and T % Tt == 0 and (long_seq or Tt == BUF_ROWS)
    tiles_per_seq = T // Tt if long_seq else 1
    seq = (lambda i: i // tiles_per_seq) if long_seq else (lambda i: i)
    colspec = lambda rows: pl.BlockSpec((rows, tf), lambda i, j: (0, j))
    hspec = pl.BlockSpec((G, BUF_ROWS, tf), lambda i, j: (seq(i), 0, j))
    tspec = pl.BlockSpec((G, BUF_ROWS, tf), lambda i, j: (i, 0, j))
    tshape = jax.ShapeDtypeStruct((M // tm * G, BUF_ROWS, ffp), F32)
    y, tail_g, tail_v = pl.pallas_call(
        functools.partial(_ffn_kernel, G=G, Tt=Tt, nj=nj, tiles_per_seq=tiles_per_seq),
        grid=(M // tm, nj),
        in_specs=[pl.BlockSpec((tm, D), lambda i, j: (i, 0)),
                  colspec(D), colspec(D),
                  colspec(FFN_CONV), colspec(FFN_CONV),
                  colspec(1), colspec(1),
                  hspec, hspec,
                  pl.BlockSpec((tf, D), lambda i, j: (j, 0)),
                  pl.BlockSpec((1, D), lambda i, j: (0, 0)),
                  pl.BlockSpec((1, D), lambda i, j: (0, 0))],
        out_specs=[pl.BlockSpec((tm, D), lambda i, j: (i, 0)), tspec, tspec],
        out_shape=[jax.ShapeDtypeStruct((M, D), F32), tshape, tshape],
        scratch_shapes=[pltpu.VMEM((tm, D), BF16),
                        pltpu.VMEM((tm, D), F32),
                        pltpu.VMEM((2, G, BUF_ROWS + Tt, tf), F32),
                        pltpu.VMEM((nj, 2, BUF_ROWS, tf), F32)],
        compiler_params=_cparams("arbitrary", "arbitrary"),
        name="conv_ffn",
    )(x, w_up[0], w_up[1], conv_w[0], conv_w[1], conv_b[0].reshape(1, ffp), conv_b[1].reshape(1, ffp),
      hist[0], hist[1], w_down, g.reshape(1, D), b.reshape(1, D))
    last = slice(tiles_per_seq - 1, None, tiles_per_seq)
    return y, tail_g[last], tail_v[last]


def _lane_col(x, lane_iota, idx):
    return jnp.sum(jnp.where(lane_iota == idx, x, 0.0), axis=1, keepdims=True)


def _delta_kernel(q_ref, k_ref, v_ref, gate_ref, ab_ref, s0_ref, alog_ref, dtb_ref, nw_ref,
                  o_ref, sout_ref, s_scr, *, C, nchunks):
    n = pl.program_id(1)

    @pl.when(n == 0)
    def _():
        s_scr[...] = s0_ref[...]

    nb = q_ref.shape[0]
    H = nb * DN_HEADS
    gh = min(DN_HEADS, LANES // C)
    R = gh * C
    groups = [list(range(g * gh, (g + 1) * gh)) for g in range(H // gh)]
    seq = lambda h: h // DN_HEADS
    hsl = lambda h: slice((h % DN_HEADS) * HEAD_DIM, (h % DN_HEADS + 1) * HEAD_DIM)
    rsl = lambda j: slice(j * C, (j + 1) * C)
    cat0 = lambda xs: xs[0] if len(xs) == 1 else jnp.concatenate(xs, axis=0)

    row = lax.broadcasted_iota(jnp.int32, (R, R), 0)
    col = lax.broadcasted_iota(jnp.int32, (R, R), 1)
    same = (row // C) == (col // C)
    eye = row == col
    incl = same & (row >= col)
    strict = same & (row > col)
    eyef = jnp.where(eye, 1.0, 0.0).astype(F32)
    lane = lax.broadcasted_iota(jnp.int32, (C, LANES), 1)
    rowhead = lax.broadcasted_iota(jnp.int32, (R, HEAD_DIM), 0) // C

    trow = lax.broadcasted_iota(jnp.int32, (C, C), 0)
    tcol = lax.broadcasted_iota(jnp.int32, (C, C), 1)
    tril = jnp.where(trow >= tcol, 1.0, 0.0).astype(F32)
    gc_all, beta_all = [], []
    for s in range(nb):
        ab = ab_ref[s]
        z = ab + dtb_ref[...]
        softplus = jnp.maximum(z, 0.0) + jnp.log1p(jnp.exp(-jnp.abs(z)))
        g_all = -jnp.exp(alog_ref[...]) * softplus
        beta_all.append(_sigmoid(ab))
        if C >= 2 * SUBLANES:
            gc_all.append(_dot3(tril, g_all))
        else:
            gc_all.append(jnp.dot(tril, g_all, preferred_element_type=F32, precision=lax.Precision.HIGHEST))

    q = [cat0([q_ref[seq(h), :, hsl(h)] for h in hs]) for hs in groups]
    k = [cat0([k_ref[seq(h), :, hsl(h)] for h in hs]) for hs in groups]
    v = [cat0([v_ref[seq(h), :, hsl(h)] for h in hs]) for hs in groups]
    gcol_h = [_lane_col(gc_all[seq(h)], lane, h % DN_HEADS) for h in range(H)]
    gcol = [cat0([gcol_h[h] for h in hs]) for hs in groups]
    bcol = [cat0([_lane_col(beta_all[seq(h)], lane, DN_HEADS + h % DN_HEADS) for h in hs]) for hs in groups]
    glast_h = [g[C - 1:C, :] for g in gcol_h]
    glast = [cat0([jnp.broadcast_to(glast_h[h], (C, 1)) for h in hs]) for hs in groups]
    grow = [jnp.sum(jnp.where(eye, g, 0.0), axis=0, keepdims=True) for g in gcol]
    decay = [jnp.where(incl, jnp.exp(jnp.where(incl, gc - gr, 0.0)), 0.0) for gc, gr in zip(gcol, grow)]
    kb = [ki * bi for ki, bi in zip(k, bcol)]
    kq = [_dot_nt(jnp.concatenate([kbi, qi], axis=0).astype(BF16), ki.astype(BF16))
          for kbi, qi, ki in zip(kb, q, k)]
    m = [jnp.where(strict, x[:R] * d, 0.0) for x, d in zip(kq, decay)]
    attn = [x[R:] * d for x, d in zip(kq, decay)]

    nm = [-x for x in m]
    p = [eyef + x for x in nm]
    nk = [_dot3(x, x) for x in nm]
    terms = 2
    while terms * 2 < C:
        res = [_dot3(jnp.concatenate([pi, ni], axis=0), ni) for pi, ni in zip(p, nk)]
        p = [pi + r[:R] for pi, r in zip(p, res)]
        nk = [r[R:] for r in res]
        terms *= 2
    p = [pi + _dot3(pi, ni) for pi, ni in zip(p, nk)]

    eg = [jnp.exp(g) for g in gcol]
    uw = [_dot(pi.astype(BF16), jnp.concatenate([vi * bi, kbi * e], axis=1).astype(BF16))
          for pi, vi, bi, kbi, e in zip(p, v, bcol, kb, eg)]
    qg = [qi * e for qi, e in zip(q, eg)]
    kg = [ki * jnp.exp(gl - gc) for ki, gl, gc in zip(k, glast, gcol)]

    s_old = [s_scr[seq(h), h % DN_HEADS] for h in range(H)]
    sb = [s.astype(BF16) for s in s_old]
    ws = []
    for g, hs in enumerate(groups):
        for j, h in enumerate(hs):
            lhs = jnp.concatenate([uw[g][rsl(j), HEAD_DIM:], qg[g][rsl(j), :]], axis=0).astype(BF16)
            ws.append(_dot(lhs, sb[h]))
    v_new = [cat0([uw[g][rsl(j), :HEAD_DIM] - ws[h][:C] for j, h in enumerate(hs)])
             for g, hs in enumerate(groups)]
    vnb = [x.astype(BF16) for x in v_new]
    o_all = [cat0([ws[h][C:] for h in hs]) + _dot(a.astype(BF16), vb)
             for hs, a, vb in zip(groups, attn, vnb)]
    for g, hs in enumerate(groups):
        for j, h in enumerate(hs):
            kgm = jnp.where(rowhead == j, kg[g], 0.0).astype(BF16) if gh > 1 else kg[g].astype(BF16)
            s_scr[seq(h), h % DN_HEADS] = s_old[h] * jnp.exp(glast_h[h]) + _dot_tn(kgm, vnb[g])
    for g, hs in enumerate(groups):
        for j, h in enumerate(hs):
            o = o_all[g][rsl(j), :]
            o = o * lax.rsqrt(jnp.mean(o * o, axis=-1, keepdims=True) + NORM_EPS) * nw_ref[...]
            gate = gate_ref[seq(h), :, hsl(h)]
            o_ref[seq(h), :, hsl(h)] = (o * (gate * _sigmoid(gate))).astype(o_ref.dtype)

    @pl.when(n == nchunks - 1)
    def _():
        sout_ref[...] = s_scr[...]


def _delta(qkvn, proj, ab, s0, a_log, dt_bias, norm_w, *, C, nb):
    B, T, _ = qkvn.shape
    nchunks = T // C
    W = DN_HEADS * HEAD_DIM
    pad = LANES - DN_HEADS
    alog_p = jnp.pad(a_log.astype(F32), (0, pad)).reshape(1, LANES)
    dtb_p = jnp.pad(dt_bias.astype(F32), (0, pad)).reshape(1, LANES)
    tok = lambda col, width: pl.BlockSpec((nb, C, width), lambda b, n: (b, n, col))
    state = pl.BlockSpec((nb, DN_HEADS, HEAD_DIM, HEAD_DIM), lambda b, n: (b, 0, 0, 0))
    return pl.pallas_call(
        functools.partial(_delta_kernel, C=C, nchunks=nchunks),
        grid=(B // nb, nchunks),
        in_specs=[tok(0, W), tok(1, W), tok(2, W), tok(3, W), tok(0, LANES), state,
                  pl.BlockSpec((1, LANES), lambda b, n: (0, 0)),
                  pl.BlockSpec((1, LANES), lambda b, n: (0, 0)),
                  pl.BlockSpec((1, HEAD_DIM), lambda b, n: (0, 0))],
        out_specs=[tok(0, W), state],
        out_shape=[jax.ShapeDtypeStruct((B, T, W), BF16),
                   jax.ShapeDtypeStruct((B, DN_HEADS, HEAD_DIM, HEAD_DIM), F32)],
        scratch_shapes=[pltpu.VMEM((nb, DN_HEADS, HEAD_DIM, HEAD_DIM), F32)],
        compiler_params=_cparams("parallel", "arbitrary"),
        name="delta_rule",
    )(qkvn, qkvn, qkvn, proj, ab, s0, alog_p, dtb_p, norm_w.reshape(1, HEAD_DIM))


def _heads_to_lanes(ref, nheads, ntok):
    return jnp.concatenate(
        [ref[pl.ds(h, ntok, stride=nheads), :].astype(BF16) for h in range(nheads)], axis=1)


def _sb_blocks(qb, kblks, vblks, bias, carry, masks, ustrict):
    zs = [_dot_nt(qb, kb) * ATT_SCALE + bias for kb in kblks]
    lks = [jnp.minimum(-z, 0.0) - jnp.log(1.0 + jnp.exp(-jnp.abs(z))) for z in zs]
    lss = [z + lk for z, lk in zip(zs, lks)]
    lks = [lk if mk is None else jnp.where(mk, lk, 0.0) for lk, mk in zip(lks, masks)]
    ustack = jnp.concatenate([ustrict, ustrict], axis=0)
    rs = [_dot(jnp.concatenate(_split2(lk), axis=1), ustack) for lk in lks]
    sums = [jnp.sum(lk, axis=1, keepdims=True) for lk in lks]
    acc = None
    for ls, r, sm, mk, vb in zip(lss, rs, sums, masks, vblks):
        a = jnp.exp(ls + r + carry)
        if mk is not None:
            a = jnp.where(mk, a, 0.0)
        pv = _dot(a.astype(BF16), vb)
        acc = pv if acc is None else acc + pv
        carry = carry + sm
    return acc, carry


def _sbp_kernel(bias_ref, q_ref, k_ref, v_ref, o_ref, kb_scr, vb_scr, *, tq):
    h = pl.program_id(1)
    qi = pl.program_id(2)

    @pl.when(qi == 0)
    def _():
        kb_scr[...] = k_ref[...].astype(BF16)
        vb_scr[...] = v_ref[...].astype(BF16)

    qb = q_ref[...].astype(BF16)
    bias = bias_ref[h]
    row = lax.broadcasted_iota(jnp.int32, (tq, tq), 0)
    col = lax.broadcasted_iota(jnp.int32, (tq, tq), 1)
    ustrict = jnp.where(row > col, 1.0, 0.0).astype(BF16)

    def blocks(kjs, carry, masks):
        starts = [kj * tq if isinstance(kj, int) else pl.multiple_of(kj * tq, tq) for kj in kjs]
        return _sb_blocks(qb, [kb_scr[pl.ds(s, tq), :] for s in starts],
                          [vb_scr[pl.ds(s, tq), :] for s in starts], bias, carry, masks, ustrict)

    acc, carry = blocks([qi], jnp.zeros((tq, 1), F32), [col < row])

    def group(n, first_kj):
        def run(st):
            acc, carry = st
            pv, carry = blocks([first_kj - d for d in range(n)], carry, [None] * n)
            return acc + pv, carry
        return run

    acc, carry = lax.fori_loop(0, qi // 4, lambda it, st: group(4, qi - 1 - 4 * it)(st), (acc, carry))
    rest = qi % 4
    acc, carry = lax.cond(rest >= 2, group(2, rest - 1), lambda st: st, (acc, carry))
    acc, _ = lax.cond(rest % 2 == 1, group(1, 0), lambda st: st, (acc, carry))
    o_ref[...] = acc.astype(o_ref.dtype)


def _sb_prompt(proj, sb_bias, *, B, T, qcol, kcol, vcol, tq):
    nq = T // tq
    grid_spec = pltpu.PrefetchScalarGridSpec(
        num_scalar_prefetch=1,
        grid=(B, SB_HEADS, nq),
        in_specs=[pl.BlockSpec((tq, HEAD_DIM), lambda b, h, i, s: (b * nq + i, qcol + h)),
                  pl.BlockSpec((T, HEAD_DIM), lambda b, h, i, s: (b, kcol + h)),
                  pl.BlockSpec((T, HEAD_DIM), lambda b, h, i, s: (b, vcol + h))],
        out_specs=pl.BlockSpec((tq, HEAD_DIM), lambda b, h, i, s: (b * nq + i, h)),
        scratch_shapes=[pltpu.VMEM((T, HEAD_DIM), BF16), pltpu.VMEM((T, HEAD_DIM), BF16)],
    )
    return pl.pallas_call(
        functools.partial(_sbp_kernel, tq=tq),
        grid_spec=grid_spec,
        out_shape=jax.ShapeDtypeStruct((B * T, SB_HEADS * HEAD_DIM), BF16),
        compiler_params=_cparams("parallel", "parallel", "arbitrary"),
        name="sb_prompt",
    )(sb_bias.astype(F32), proj, proj, proj)


def _sbs_kernel(pt_ref, q_ref, kn_ref, vn_ref, *rest, T, nsteps, pps):
    kc_refs, vc_refs = rest[:pps], rest[pps:2 * pps]
    bias_ref, o_ref, qbd_scr, knp_scr, vnp_scr, acc_scr, carry_scr = rest[2 * pps:]
    p = pl.program_id(1)
    R = SB_HEADS * T
    W = SB_HEADS * HEAD_DIM
    rrow = lax.broadcasted_iota(jnp.int32, (R, W), 0)
    rcol = lax.broadcasted_iota(jnp.int32, (R, W), 1)
    own_head = (rrow // T) == (rcol // HEAD_DIM)
    row = lax.broadcasted_iota(jnp.int32, (PAGE_SIZE, PAGE_SIZE), 0)
    col = lax.broadcasted_iota(jnp.int32, (PAGE_SIZE, PAGE_SIZE), 1)
    ustrict = jnp.where(row > col, 1.0, 0.0).astype(BF16)
    bias = bias_ref[...]

    def step(kblks, vblks, masks):
        pv, carry = _sb_blocks(qbd_scr[...], kblks, vblks, bias, carry_scr[...], masks, ustrict)
        acc_scr[...] += pv
        carry_scr[...] = carry

    @pl.when(p == 0)
    def _():
        q = q_ref[...]
        qrep = jnp.concatenate([q] * SB_HEADS, axis=0)
        qbd_scr[...] = jnp.where(own_head, qrep, 0.0).astype(BF16)
        knp_scr[...] = jnp.zeros_like(knp_scr)
        vnp_scr[...] = jnp.zeros_like(vnp_scr)
        knp_scr[0:T, :] = kn_ref[...]
        vnp_scr[0:T, :] = vn_ref[...]
        acc_scr[...] = jnp.zeros_like(acc_scr)
        carry_scr[...] = jnp.zeros_like(carry_scr)
        qpos = lax.broadcasted_iota(jnp.int32, (R, PAGE_SIZE), 0) % T
        kpos = lax.broadcasted_iota(jnp.int32, (R, PAGE_SIZE), 1)
        step([knp_scr[...].astype(BF16)], [vnp_scr[...].astype(BF16)], [kpos < qpos])

    step([_heads_to_lanes(r.at[0], SB_HEADS, PAGE_SIZE) for r in kc_refs],
         [_heads_to_lanes(r.at[0], SB_HEADS, PAGE_SIZE) for r in vc_refs], [None] * pps)

    @pl.when(p == nsteps - 1)
    def _():
        accm = jnp.where(own_head, acc_scr[...], 0.0)
        out = accm[0:T]
        for h in range(1, SB_HEADS):
            out = out + accm[h * T:(h + 1) * T]
        o_ref[...] = out.astype(o_ref.dtype)


def _sb_sample(proj, cache_k, cache_v, page_table, sb_bias, *, B, T, qcol, kcol, vcol):
    npages = page_table.shape[1]
    pps = SB_PAGES_PER_STEP if npages % SB_PAGES_PER_STEP == 0 else 1
    nsteps = npages // pps
    W = SB_HEADS * HEAD_DIM
    R = SB_HEADS * T
    bias_rows = jnp.broadcast_to(jnp.repeat(sb_bias.astype(F32), T)[:, None], (R, PAGE_SIZE))

    def page_spec(i):
        return pl.BlockSpec((1, PAGE_SIZE * SB_HEADS, HEAD_DIM),
                            lambda b, p, pt: (pt[b, npages - 1 - (p * pps + i)], 0, 0))

    grid_spec = pltpu.PrefetchScalarGridSpec(
        num_scalar_prefetch=1,
        grid=(B, nsteps),
        in_specs=[pl.BlockSpec((T, W), lambda b, p, pt: (b, qcol)),
                  pl.BlockSpec((T, W), lambda b, p, pt: (b, kcol)),
                  pl.BlockSpec((T, W), lambda b, p, pt: (b, vcol))]
                 + [page_spec(i) for i in range(pps)] * 2
                 + [pl.BlockSpec((R, PAGE_SIZE), lambda b, p, pt: (0, 0))],
        out_specs=pl.BlockSpec((T, W), lambda b, p, pt: (b, 0)),
        scratch_shapes=[pltpu.VMEM((R, W), BF16),
                        pltpu.VMEM((PAGE_SIZE, W), F32),
                        pltpu.VMEM((PAGE_SIZE, W), F32),
                        pltpu.VMEM((R, W), F32),
                        pltpu.VMEM((R, 1), F32)],
    )
    return pl.pallas_call(
        functools.partial(_sbs_kernel, T=T, nsteps=nsteps, pps=pps),
        grid_spec=grid_spec,
        out_shape=jax.ShapeDtypeStruct((B * T, W), BF16),
        compiler_params=_cparams("arbitrary", "arbitrary"),
        name="sb_sample",
    )(page_table, proj, proj, proj, *([cache_k] * pps), *([cache_v] * pps), bias_rows)


def _xattn_prompt_kernel(q_ref, k_ref, v_ref, o_ref):
    for h in range(X_HEADS):
        sl = slice(h * HEAD_DIM, (h + 1) * HEAD_DIM)
        s = _dot_nt(q_ref[:, sl].astype(BF16), k_ref[0, :, sl].astype(BF16)) * ATT_SCALE
        p = jnp.exp(s - jnp.max(s, axis=-1, keepdims=True))
        p = p / jnp.sum(p, axis=-1, keepdims=True)
        o_ref[:, sl] = _dot(p.astype(BF16), v_ref[0, :, sl].astype(BF16)).astype(o_ref.dtype)


def _xattn_prompt(q, mk, mv, *, B, T, tq):
    W = X_HEADS * HEAD_DIM
    nq = T // tq
    nm = mk.shape[1]
    return pl.pallas_call(
        _xattn_prompt_kernel,
        grid=(B, nq),
        in_specs=[pl.BlockSpec((tq, W), lambda b, i: (b * nq + i, 0)),
                  pl.BlockSpec((1, nm, W), lambda b, i: (b, 0, 0)),
                  pl.BlockSpec((1, nm, W), lambda b, i: (b, 0, 0))],
        out_specs=pl.BlockSpec((tq, W), lambda b, i: (b * nq + i, 0)),
        out_shape=jax.ShapeDtypeStruct((B * T, W), BF16),
        compiler_params=_cparams("parallel", "parallel"),
        name="xattn_prompt",
    )(q, mk, mv)


def _xattn_sample_kernel(q_ref, k_ref, v_ref, o_ref, *, G, T, nm):
    R = X_HEADS * T
    W = X_HEADS * HEAD_DIM
    rrow = lax.broadcasted_iota(jnp.int32, (R, W), 0)
    rcol = lax.broadcasted_iota(jnp.int32, (R, W), 1)
    own_head = (rrow // T) == (rcol // HEAD_DIM)
    for g in range(G):
        q = q_ref[g]
        qbd = jnp.where(own_head, jnp.concatenate([q] * X_HEADS, axis=0), 0.0).astype(BF16)
        s = _dot_nt(qbd, _heads_to_lanes(k_ref.at[g], X_HEADS, nm)) * ATT_SCALE
        p = jnp.exp(s - jnp.max(s, axis=-1, keepdims=True))
        p = p / jnp.sum(p, axis=-1, keepdims=True)
        of = jnp.where(own_head, _dot(p.astype(BF16), _heads_to_lanes(v_ref.at[g], X_HEADS, nm)), 0.0)
        out = of[0:T]
        for h in range(1, X_HEADS):
            out = out + of[h * T:(h + 1) * T]
        o_ref[g] = out.astype(o_ref.dtype)


def _xattn_sample(q3, mk, mv, layer, *, G):
    B, T, W = q3.shape
    nm = mk.shape[2] // X_HEADS
    return pl.pallas_call(
        functools.partial(_xattn_sample_kernel, G=G, T=T, nm=nm),
        grid=(B // G,),
        in_specs=[pl.BlockSpec((G, T, W), lambda b: (b, 0, 0)),
                  pl.BlockSpec((None, G, nm * X_HEADS, HEAD_DIM), lambda b: (layer, b, 0, 0)),
                  pl.BlockSpec((None, G, nm * X_HEADS, HEAD_DIM), lambda b: (layer, b, 0, 0))],
        out_specs=pl.BlockSpec((G, T, W), lambda b: (b, 0, 0)),
        out_shape=jax.ShapeDtypeStruct((B, T, W), BF16),
        compiler_params=_cparams("parallel"),
        name="xattn_sample",
    )(q3, mk, mv)


def _lru_kernel(y_ref, xc_ref, wr_ref, wi_ref, br_ref, bi_ref, lam_ref, h0_ref, o_ref, hl_ref,
                hc_scr, a_scr, b_scr, h_scr, *, G, Tt, nt):
    t = pl.program_id(1)
    rows = G * Tt
    ngroups = rows // SUBLANES
    tc = a_scr.shape[1]
    wb = tc // LRU_BLOCKS

    @pl.when(t == 0)
    def _():
        hc_scr[...] = h0_ref[...]

    xc = xc_ref[...].reshape(rows, tc)
    xbs = [xc[:, n * wb:(n + 1) * wb].astype(BF16) for n in range(LRU_BLOCKS)]
    gate = lambda w_ref, b_ref: _sigmoid(
        jnp.concatenate([_dot(xbs[n], w_ref[n]) for n in range(LRU_BLOCKS)], axis=1) + b_ref[...])
    r = gate(wr_ref, br_ref)
    i = gate(wi_ref, bi_ref)
    log_a = LRU_C * r * _neg_softplus(-lam_ref[...])
    a = jnp.exp(log_a)
    a_scr[...] = a
    b_scr[...] = jnp.sqrt(-jnp.tanh(log_a) * (a * a + 1.0)) * (i * xc)
    sub = lax.broadcasted_iota(jnp.int32, (SUBLANES, tc), 0)

    def group(gi, carry):
        r0 = pl.multiple_of(gi * SUBLANES, SUBLANES)
        a = a_scr[pl.ds(r0, SUBLANES), :]
        b = b_scr[pl.ds(r0, SUBLANES), :]
        for s in (1, 2, 4):
            keep = sub >= s
            a_sh = jnp.where(keep, pltpu.roll(a, s, 0), 1.0)
            b_sh = jnp.where(keep, pltpu.roll(b, s, 0), 0.0)
            b = a * b_sh + b
            a = a * a_sh
        hprev = hc_scr[gi] if Tt == SUBLANES else carry
        h8 = b + a * hprev
        h_scr[pl.ds(r0, SUBLANES), :] = h8
        last = h8[SUBLANES - 1:SUBLANES, :]
        if Tt == SUBLANES:
            hl_ref[gi] = last
            return carry
        return last

    carry = lax.fori_loop(0, ngroups, group, hc_scr[0])
    if Tt != SUBLANES:
        hc_scr[0] = carry

        @pl.when(t == nt - 1)
        def _():
            hl_ref[0] = carry

    y = y_ref[...].reshape(rows, tc)
    o_ref[...] = (jax.nn.gelu(y) * h_scr[...]).astype(o_ref.dtype).reshape(G, Tt, tc)


def _lru(proj3, xc3, w_r, w_i, b_r, b_i, lam, h0, *, G, Tt):
    B, T, D = xc3.shape
    wb = D // LRU_BLOCKS
    nt = T // Tt
    assert Tt % SUBLANES == 0 and (G == 1 or Tt == SUBLANES)
    vec = lambda a: a.astype(F32).reshape(1, D)
    whole = lambda shape: pl.BlockSpec(shape, lambda b, t: (0,) * len(shape))
    return pl.pallas_call(
        functools.partial(_lru_kernel, G=G, Tt=Tt, nt=nt),
        grid=(B // G, nt),
        in_specs=[pl.BlockSpec((G, Tt, D), lambda b, t: (b, t, 0)),
                  pl.BlockSpec((G, Tt, D), lambda b, t: (b, t, 0)),
                  whole((LRU_BLOCKS, wb, wb)), whole((LRU_BLOCKS, wb, wb)),
                  whole((1, D)), whole((1, D)), whole((1, D)),
                  pl.BlockSpec((G, 1, D), lambda b, t: (b, 0, 0))],
        out_specs=[pl.BlockSpec((G, Tt, D), lambda b, t: (b, t, 0)),
                   pl.BlockSpec((G, 1, D), lambda b, t: (b, 0, 0))],
        out_shape=[jax.ShapeDtypeStruct((B, T, D), BF16),
                   jax.ShapeDtypeStruct((B, 1, D), F32)],
        scratch_shapes=[pltpu.VMEM((G, 1, D), F32),
                        pltpu.VMEM((G * Tt, D), F32),
                        pltpu.VMEM((G * Tt, D), F32),
                        pltpu.VMEM((G * Tt, D), F32)],
        compiler_params=_cparams("parallel", "arbitrary"),
        name="rglru",
    )(proj3, xc3, w_r, w_i, vec(b_r), vec(b_i), vec(lam), h0.reshape(B, 1, D))


def _pad_buf(buf):
    return jnp.pad(buf, ((0, 0), (BUF_ROWS - buf.shape[1], 0), (0, 0)))


def _row_tile(m):
    return 512 if m % 512 == 0 else m


def _mm_row_tile(m):
    return 1024 if m % 1024 == 0 else _row_tile(m)


def _col_tile(n):
    return 1024 if n % 1024 == 0 else 512 if n % 512 == 0 else n


def _wblocks(w):
    return w.astype(BF16)


class _Stream:
    def __init__(self, B, T):
        self.B, self.T = B, T
        self.M = B * T
        self.long = T > SUBLANES
        self.G = 1 if self.long else B


def _ab_layer(st, x, w, dn_conv, dn_S, sb_fn):
    B, T, M = st.B, st.T, st.M
    tm = _mm_row_tile(M)
    proj = _mm(x, w["ab_main"], tm=tm)
    ab = _mm(x, w["ab_ab"], tm=tm)
    proj3 = proj.reshape(B, T, -1)
    nqkv = 3 * DN_HEADS * HEAD_DIM
    qkvn = _conv_dn(proj3, _pad_buf(dn_conv), w["ab_conv_w"], G=st.G)
    C = 64 if st.long else T
    o_dn, s_new = _delta(qkvn, proj3, ab.reshape(B, T, LANES), dn_S, w["dn_A_log"], w["dn_dt_bias"],
                         w["dn_norm_w"], C=C, nb=2 if B % 2 == 0 else 1)
    o_sb = sb_fn(proj)
    o = jnp.concatenate([o_dn.reshape(M, -1), o_sb], axis=-1)
    new_conv = proj3[:, T - (DN_CONV - 1):, :nqkv]
    W = SB_HEADS * HEAD_DIM
    k_new = proj[:, 5 * W:6 * W].reshape(B, T, SB_HEADS, HEAD_DIM)
    v_new = proj[:, 6 * W:7 * W].reshape(B, T, SB_HEADS, HEAD_DIM)
    return o, w["ab_w_out"], (k_new, v_new, new_conv, s_new)


def _lru_layer(st, x, w, lru_conv, lru_h):
    B, T, M = st.B, st.T, st.M
    D = x.shape[1]
    proj = _mm(x, w["c_w_in"], tm=_mm_row_tile(M))
    proj3 = proj.reshape(B, T, 2 * D)
    tc = 256
    xc3 = _conv_lin(proj3, D // tc, _pad_buf(lru_conv), w["lru_conv_w"], w["lru_conv_b"], G=st.G, tc=tc)
    G, Tt = (1, 256) if st.long else (min(32, B), T)
    o3, h_last = _lru(proj3, xc3, w["lru_w_r"], w["lru_w_i"], w["lru_b_r"], w["lru_b_i"],
                      w["lru_lambda"], lru_h, G=G, Tt=Tt)
    new_conv = proj3[:, T - (LRU_CONV - 1):, D:]
    return o3.reshape(M, D), w["c_w_out"], (new_conv, h_last.reshape(B, D))


def _xattn_layer(st, x, w, mk3, mv3, layer):
    B, T, M = st.B, st.T, st.M
    q = _mm(x, w["xa_w_q"], tm=_mm_row_tile(M))
    if st.long:
        return _xattn_prompt(q, mk3, mv3, B=B, T=T, tq=512)
    W = q.shape[1]
    return _xattn_sample(q.reshape(B, T, W), mk3, mv3, layer, G=8).reshape(M, W)


def _ffn_layer(st, x, w, ffn_conv, d_ff, g, b):
    B, T, M = st.B, st.T, st.M
    ffp = w["ffn_w_down"].shape[0]
    halves = lambda a: (a[..., :d_ff], a[..., d_ff:])
    padc = lambda a: jnp.pad(a, [(0, 0)] * (a.ndim - 1) + [(0, ffp - d_ff)])
    hist = tuple(_pad_buf(padc(h)) for h in halves(ffn_conv))
    y, tail_g, tail_v = _ffn(x, w["ffn_w_up"], w["ffn_conv_w"], w["ffn_conv_b"], hist,
                             w["ffn_w_down"], g, b, B=B, T=T, tm=_row_tile(M), tf=FFN_COL_TILE)
    keep = slice(BUF_ROWS - (FFN_CONV - 1), BUF_ROWS)
    new_conv = jnp.concatenate([tail_g[:, keep, :d_ff], tail_v[:, keep, :d_ff]], axis=-1)
    return y, new_conv


def kernel(x_prompt, x_sample, mem_prompt, cache_sb_k, cache_sb_v, cache_mem_k, cache_mem_v, state_dn_conv, state_dn_S, state_lru_conv, state_lru_h, state_ffn_conv, page_table, ab_w_in, ab_conv_w, dn_A_log, dn_dt_bias, dn_norm_w, sb_bias, ab_w_out, c_w_in, lru_conv_w, lru_conv_b, lru_w_r, lru_b_r, lru_w_i, lru_b_i, lru_lambda, c_w_out, xa_w_q, xa_w_k, xa_w_v, xa_w_o, ffn_w_up, ffn_conv_w, ffn_conv_b, ffn_w_down, ln_g, ln_b):
    bp, tp, d_model = x_prompt.shape
    bs, ts, _ = x_sample.shape
    sp, ss = _Stream(bp, tp), _Stream(bs, ts)
    dt = x_prompt.dtype
    n_mem = mem_prompt.shape[1]
    xw = X_HEADS * HEAD_DIM
    nqkv = 3 * DN_HEADS * HEAD_DIM
    d_ff = ffn_w_down.shape[1]
    ffp = -(-d_ff // FFN_COL_TILE) * FFN_COL_TILE
    depth = ln_g.shape[0]

    xp = x_prompt.reshape(sp.M, d_model)
    xs = x_sample.reshape(ss.M, d_model)
    mem2 = mem_prompt.reshape(bp * n_mem, d_model)

    outs_p = {k: [] for k in ("sb_k", "sb_v", "dn_conv", "dn_S", "lru_conv", "lru_h", "mem_k", "mem_v", "ffn_conv")}
    outs_s = {k: [] for k in ("sb_k", "sb_v", "dn_conv", "dn_S", "lru_conv", "lru_h", "ffn_conv")}

    def pad_ff(a, axis):
        pad = [(0, 0)] * a.ndim
        pad[axis] = (0, ffp - d_ff)
        return jnp.pad(a, pad)

    for layer in range(depth):
        w = {}
        if layer % 2 == 0:
            e = layer // 2
            w_in = ab_w_in[e]
            c_ab = nqkv + DN_HEADS * HEAD_DIM
            w["ab_main"] = _wblocks(jnp.concatenate([w_in[:, :c_ab], w_in[:, c_ab + 2 * DN_HEADS:]], axis=1))
            w["ab_ab"] = _wblocks(jnp.pad(w_in[:, c_ab:c_ab + 2 * DN_HEADS], ((0, 0), (0, LANES - 2 * DN_HEADS))))
            w["ab_conv_w"] = ab_conv_w[e]
            w["dn_A_log"], w["dn_dt_bias"], w["dn_norm_w"] = dn_A_log[e], dn_dt_bias[e], dn_norm_w[e]
            w["ab_w_out"] = ab_w_out[e].astype(BF16)
        else:
            o = layer // 2
            w["c_w_in"] = _wblocks(c_w_in[o])
            w["lru_conv_w"], w["lru_conv_b"] = lru_conv_w[o], lru_conv_b[o]
            w["lru_w_r"], w["lru_w_i"] = lru_w_r[o].astype(BF16), lru_w_i[o].astype(BF16)
            w["lru_b_r"], w["lru_b_i"], w["lru_lambda"] = lru_b_r[o], lru_b_i[o], lru_lambda[o]
            w["c_w_out"] = c_w_out[o].astype(BF16)
        w["xa_w_q"] = _wblocks(xa_w_q[layer])
        w["ffn_w_up"] = (pad_ff(ffn_w_up[layer][:, :d_ff].astype(BF16), 1),
                         pad_ff(ffn_w_up[layer][:, d_ff:].astype(BF16), 1))
        w["ffn_conv_w"] = (pad_ff(ffn_conv_w[layer][:, :d_ff], 1), pad_ff(ffn_conv_w[layer][:, d_ff:], 1))
        w["ffn_conv_b"] = (pad_ff(ffn_conv_b[layer][:d_ff], 0), pad_ff(ffn_conv_b[layer][d_ff:], 0))
        w["ffn_w_down"] = pad_ff(ffn_w_down[layer].astype(BF16), 0)
        w_xo = xa_w_o[layer].astype(BF16)

        if layer % 2 == 0:
            e = layer // 2
            qcol, kcol, vcol = 4 * DN_HEADS, 5 * DN_HEADS, 6 * DN_HEADS
            sbp = functools.partial(_sb_prompt, sb_bias=sb_bias[e], B=bp, T=tp, qcol=qcol, kcol=kcol, vcol=vcol, tq=256)
            op, wout, (k_new, v_new, cb, s_new) = _ab_layer(
                sp, xp, w, jnp.zeros((bp, DN_CONV - 1, nqkv), dt),
                jnp.zeros((bp, DN_HEADS, HEAD_DIM, HEAD_DIM), F32), sbp)
            outs_p["sb_k"].append(k_new); outs_p["sb_v"].append(v_new)
            outs_p["dn_conv"].append(cb); outs_p["dn_S"].append(s_new)
            n_pool = cache_sb_k.shape[1]
            ck = cache_sb_k[e].reshape(n_pool, PAGE_SIZE * SB_HEADS, HEAD_DIM)
            cv = cache_sb_v[e].reshape(n_pool, PAGE_SIZE * SB_HEADS, HEAD_DIM)
            sbs = functools.partial(_sb_sample, cache_k=ck, cache_v=cv, page_table=page_table, sb_bias=sb_bias[e],
                                    B=bs, T=ts, qcol=4, kcol=5, vcol=6)
            os_, _, (k_new, v_new, cb, s_new) = _ab_layer(ss, xs, w, state_dn_conv[e], state_dn_S[e], sbs)
            outs_s["sb_k"].append(k_new); outs_s["sb_v"].append(v_new)
            outs_s["dn_conv"].append(cb); outs_s["dn_S"].append(s_new)
        else:
            o = layer // 2
            op, wout, (cb, hl) = _lru_layer(sp, xp, w, jnp.zeros((bp, LRU_CONV - 1, d_model), dt),
                                            jnp.zeros((bp, d_model), dt))
            outs_p["lru_conv"].append(cb); outs_p["lru_h"].append(hl)
            os_, _, (cb, hl) = _lru_layer(ss, xs, w, state_lru_conv[o], state_lru_h[o])
            outs_s["lru_conv"].append(cb); outs_s["lru_h"].append(hl)
        xp = _mm_ln(op, wout, xp, ln_g[layer, 0], ln_b[layer, 0], tm=LN_ROW_TILE // 2, tk=wout.shape[0])
        xs = _mm_ln(os_, wout, xs, ln_g[layer, 0], ln_b[layer, 0], tm=LN_ROW_TILE // 2, tk=wout.shape[0])

        mk = _mm(mem2, _wblocks(xa_w_k[layer]), tm=_row_tile(bp * n_mem))
        mv = _mm(mem2, _wblocks(xa_w_v[layer]), tm=_row_tile(bp * n_mem))
        outs_p["mem_k"].append(mk.reshape(bp, n_mem, X_HEADS, HEAD_DIM))
        outs_p["mem_v"].append(mv.reshape(bp, n_mem, X_HEADS, HEAD_DIM))
        ap = _xattn_layer(sp, xp, w, mk.reshape(bp, n_mem, xw), mv.reshape(bp, n_mem, xw), layer)
        as_ = _xattn_layer(ss, xs, w, cache_mem_k.reshape(depth, bs, n_mem * X_HEADS, HEAD_DIM),
                           cache_mem_v.reshape(depth, bs, n_mem * X_HEADS, HEAD_DIM), layer)
        xp = _mm_ln(ap, w_xo, xp, ln_g[layer, 1], ln_b[layer, 1], tm=LN_ROW_TILE, tk=xw)
        xs = _mm_ln(as_, w_xo, xs, ln_g[layer, 1], ln_b[layer, 1], tm=LN_ROW_TILE, tk=xw)

        xp, fb = _ffn_layer(sp, xp, w, jnp.zeros((bp, FFN_CONV - 1, 2 * d_ff), dt), d_ff,
                            ln_g[layer, 2], ln_b[layer, 2])
        outs_p["ffn_conv"].append(fb)
        xs, fb = _ffn_layer(ss, xs, w, state_ffn_conv[layer], d_ff, ln_g[layer, 2], ln_b[layer, 2])
        outs_s["ffn_conv"].append(fb)

    st = lambda xs_: jnp.stack(xs_)
    return (xp.reshape(bp, tp, d_model), xs.reshape(bs, ts, d_model),
            st(outs_p["sb_k"]), st(outs_p["sb_v"]), st(outs_p["dn_conv"]), st(outs_p["dn_S"]),
            st(outs_p["lru_conv"]), st(outs_p["lru_h"]), st(outs_p["mem_k"]), st(outs_p["mem_v"]),
            st(outs_p["ffn_conv"]),
            st(outs_s["sb_k"]), st(outs_s["sb_v"]), st(outs_s["dn_conv"]), st(outs_s["dn_S"]),
            st(outs_s["lru_conv"]), st(outs_s["lru_h"]), st(outs_s["ffn_conv"]))
```

```python
import functools

import jax
import jax.numpy as jnp
from jax import lax
from jax.experimental import pallas as pl
from jax.experimental.pallas import tpu as pltpu

F32 = jnp.float32
BF16 = jnp.bfloat16

HEAD_DIM = 128
DN_HEADS = 8
SB_HEADS = 8
X_HEADS = 4
DN_CONV = 4
LRU_CONV = 4
FFN_CONV = 3
LRU_BLOCKS = 8
LRU_C = 8.0
PAGE_SIZE = 128
DEPTH = 2
DEEPNORM_ALPHA = (2.0 * DEPTH) ** 0.25
LN_EPS = 1e-5
NORM_EPS = 1e-6
ATT_SCALE = HEAD_DIM ** -0.5
LOG2E = 1.4426950408889634

LANES = 128
SUBLANES = 8
VMEM_LIMIT_BYTES = 56 * 1024 * 1024
BUF_ROWS = SUBLANES
SB_PAGES_PER_STEP = 8
LN_ROW_TILE = 512
FFN_COL_TILE = 512
FFN_SUB_TILE = 256


def _cparams(*sem):
    return pltpu.CompilerParams(dimension_semantics=sem, vmem_limit_bytes=VMEM_LIMIT_BYTES)


def _dot(a, b):
    return jnp.dot(a, b, preferred_element_type=F32)


def _dot_nt(a, b):
    return lax.dot_general(a, b, (((1,), (1,)), ((), ())), preferred_element_type=F32)


def _dot_tn(a, b):
    return lax.dot_general(a, b, (((0,), (0,)), ((), ())), preferred_element_type=F32)


def _split2(x):
    hi = x.astype(BF16)
    lo = (x - hi.astype(F32)).astype(BF16)
    return hi, lo


def _dot3(a, b):
    ah, al = _split2(a)
    bh, bl = _split2(b)
    return _dot(ah, bh) + _dot(ah, bl) + _dot(al, bh)


def _sigmoid(x):
    return 1.0 / (1.0 + jnp.exp(-x))


def _neg_softplus(z):
    return jnp.minimum(-z, 0.0) - jnp.log1p(jnp.exp(-jnp.abs(z)))


def _mm_kernel(x_ref, w_ref, o_ref, xb_ref):
    @pl.when(pl.program_id(1) == 0)
    def _():
        xb_ref[...] = x_ref[...].astype(BF16)

    o_ref[...] = _dot(xb_ref[...], w_ref[...]).astype(o_ref.dtype)


def _mm(x, w, *, tm, out_dtype=F32):
    M, K = x.shape
    N = w.shape[1]
    tn = _col_tile(N)
    assert M % tm == 0 and w.shape[0] == K
    return pl.pallas_call(
        _mm_kernel,
        grid=(M // tm, N // tn),
        in_specs=[pl.BlockSpec((tm, K), lambda i, j: (i, 0)),
                  pl.BlockSpec((K, tn), lambda i, j: (0, j))],
        out_specs=pl.BlockSpec((tm, tn), lambda i, j: (i, j)),
        out_shape=jax.ShapeDtypeStruct((M, N), out_dtype),
        scratch_shapes=[pltpu.VMEM((tm, K), BF16)],
        compiler_params=_cparams("parallel", "arbitrary"),
        name="proj",
    )(x, w)


def _mm_ln_kernel(x_ref, w_ref, r_ref, g_ref, b_ref, o_ref, acc_ref, *, nk):
    k = pl.program_id(1)
    part = _dot(x_ref[...].astype(BF16), w_ref[...])

    @pl.when(k == 0)
    def _():
        acc_ref[...] = part

    @pl.when(k > 0)
    def _():
        acc_ref[...] += part

    @pl.when(k == nk - 1)
    def _():
        h = DEEPNORM_ALPHA * r_ref[...] + acc_ref[...]
        hc = h - jnp.mean(h, axis=-1, keepdims=True)
        var = jnp.mean(hc * hc, axis=-1, keepdims=True)
        o_ref[...] = hc * lax.rsqrt(var + LN_EPS) * g_ref[...] + b_ref[...]


def _mm_ln(x, w, resid, g, b, *, tm, tk):
    M, K = x.shape
    N = w.shape[1]
    tm = min(tm, M)
    assert M % tm == 0 and K % tk == 0
    nk = K // tk
    return pl.pallas_call(
        functools.partial(_mm_ln_kernel, nk=nk),
        grid=(M // tm, nk),
        in_specs=[pl.BlockSpec((tm, tk), lambda i, k: (i, k)),
                  pl.BlockSpec((tk, N), lambda i, k: (k, 0)),
                  pl.BlockSpec((tm, N), lambda i, k: (i, 0)),
                  pl.BlockSpec((1, N), lambda i, k: (0, 0)),
                  pl.BlockSpec((1, N), lambda i, k: (0, 0))],
        out_specs=pl.BlockSpec((tm, N), lambda i, k: (i, 0)),
        out_shape=jax.ShapeDtypeStruct((M, N), F32),
        scratch_shapes=[pltpu.VMEM((tm, N), F32)],
        compiler_params=_cparams("parallel", "arbitrary"),
        name="proj_ln",
    )(x, w, resid, g.reshape(1, N), b.reshape(1, N))


def _conv_taps(u_ref, buf_ref, w_ref, ext_ref, width):
    T = u_ref.shape[1]
    ext_ref[:, 0:BUF_ROWS, :] = buf_ref[...]
    ext_ref[:, BUF_ROWS:BUF_ROWS + T, :] = u_ref[...]
    y = None
    for j in range(width):
        off = BUF_ROWS - (width - 1) + j
        term = ext_ref[:, off:off + T, :] * w_ref[j:j + 1, :][None]
        y = term if y is None else y + term
    return y


def _conv_dn_kernel(u_ref, buf_ref, w_ref, o_ref, ext_ref):
    c = pl.program_id(1)
    y = _conv_taps(u_ref, buf_ref, w_ref, ext_ref, DN_CONV)
    y = y * _sigmoid(y)
    ss = jnp.sum(y * y, axis=-1, keepdims=True)
    scale = jnp.where(c < DN_HEADS, ATT_SCALE, 1.0).astype(F32)
    yn = y * lax.rsqrt(ss + NORM_EPS) * scale
    o_ref[...] = jnp.where(c < 2 * DN_HEADS, yn, y)


def _conv_dn(u3, bufpad, w, *, G):
    B, T, _ = u3.shape
    C = w.shape[1]
    nc = C // HEAD_DIM
    return pl.pallas_call(
        _conv_dn_kernel,
        grid=(B // G, nc),
        in_specs=[pl.BlockSpec((G, T, HEAD_DIM), lambda b, c: (b, 0, c)),
                  pl.BlockSpec((G, BUF_ROWS, HEAD_DIM), lambda b, c: (b, 0, c)),
                  pl.BlockSpec((DN_CONV, HEAD_DIM), lambda b, c: (0, c))],
        out_specs=pl.BlockSpec((G, T, HEAD_DIM), lambda b, c: (b, 0, c)),
        out_shape=jax.ShapeDtypeStruct((B, T, C), F32),
        scratch_shapes=[pltpu.VMEM((G, BUF_ROWS + T, HEAD_DIM), F32)],
        compiler_params=_cparams("parallel", "parallel"),
        name="conv_dn",
    )(u3, bufpad, w)


def _conv_lin_kernel(u_ref, buf_ref, w_ref, b_ref, o_ref, ext_ref):
    y = _conv_taps(u_ref, buf_ref, w_ref, ext_ref, LRU_CONV)
    o_ref[...] = y + b_ref[...][None]


def _conv_lin(u3, col0, bufpad, w, bias, *, G, tc):
    B, T, _ = u3.shape
    C = w.shape[1]
    return pl.pallas_call(
        _conv_lin_kernel,
        grid=(B // G, C // tc),
        in_specs=[pl.BlockSpec((G, T, tc), lambda b, c: (b, 0, col0 + c)),
                  pl.BlockSpec((G, BUF_ROWS, tc), lambda b, c: (b, 0, c)),
                  pl.BlockSpec((LRU_CONV, tc), lambda b, c: (0, c)),
                  pl.BlockSpec((1, tc), lambda b, c: (0, c))],
        out_specs=pl.BlockSpec((G, T, tc), lambda b, c: (b, 0, c)),
        out_shape=jax.ShapeDtypeStruct((B, T, C), F32),
        scratch_shapes=[pltpu.VMEM((G, BUF_ROWS + T, tc), F32)],
        compiler_params=_cparams("parallel", "parallel"),
        name="conv_lin",
    )(u3, bufpad, w, bias.reshape(1, C))


def _ffn_kernel(x_ref, wg_ref, wv_ref, cwg_ref, cwv_ref, cbg_ref, cbv_ref, hg_ref, hv_ref, wd_ref,
                g_ref, b_ref, y_ref, tg_ref, tv_ref, xb_scr, acc_scr, ext_scr, carry_scr,
                *, G, Tt, nj, tiles_per_seq):
    i = pl.program_id(0)
    j = pl.program_id(1)
    tm, tf = G * Tt, ext_scr.shape[3]
    sub = min(tf, FFN_SUB_TILE)

    @pl.when(j == 0)
    def _():
        xb_scr[...] = x_ref[...].astype(BF16)
        acc_scr[...] = jnp.zeros_like(acc_scr)

    if G == 1:
        @pl.when((i == 0) & (j == 0))
        def _():
            carry_scr[...] = jnp.zeros_like(carry_scr)

    xb = xb_scr[...]
    sides = ((wg_ref, cwg_ref, cbg_ref, hg_ref, tg_ref), (wv_ref, cwv_ref, cbv_ref, hv_ref, tv_ref))
    subs = [slice(c, c + sub) for c in range(0, tf, sub)]
    ups = [[_dot(xb, side[0][:, cs]) for side in sides] for cs in subs]
    first = (i % tiles_per_seq) == 0
    part = None
    for cs, us in zip(subs, ups):
        ys = []
        for s, (u, (_, cw_ref, cb_ref, h_ref, t_ref)) in enumerate(zip(us, sides)):
            if G == 1:
                hist = jnp.where(first, h_ref[0, :, cs], carry_scr[j, s, :, cs])
                tail = u[tm - BUF_ROWS:, :]
                carry_scr[j, s, :, cs] = tail
                t_ref[0, :, cs] = tail
                ext_scr[s, 0, 0:BUF_ROWS, cs] = hist
            else:
                t_ref[:, :, cs] = u.reshape(G, Tt, sub)
                ext_scr[s, :, 0:BUF_ROWS, cs] = h_ref[:, :, cs]
            ext_scr[s, :, BUF_ROWS:BUF_ROWS + Tt, cs] = u.reshape(G, Tt, sub)
            y = None
            for t in range(FFN_CONV):
                off = BUF_ROWS - (FFN_CONV - 1) + t
                term = ext_scr[s, :, off:off + Tt, cs] * cw_ref[t:t + 1, cs][None]
                y = term if y is None else y + term
            ys.append(y + cb_ref[:, cs][None])
        yg, yv = ys
        h = (yg * _sigmoid(yg) * yv).reshape(tm, sub).astype(BF16)
        p = _dot(h, wd_ref[cs, :])
        part = p if part is None else part + p
    acc_scr[...] += part

    @pl.when(j == nj - 1)
    def _():
        hsum = DEEPNORM_ALPHA * x_ref[...] + acc_scr[...]
        hc = hsum - jnp.mean(hsum, axis=-1, keepdims=True)
        var = jnp.mean(hc * hc, axis=-1, keepdims=True)
        y_ref[...] = hc * lax.rsqrt(var + LN_EPS) * g_ref[...] + b_ref[...]


def _ffn(x, w_up, conv_w, conv_b, hist, w_down, g, b, *, B, T, tm, tf):
    M, D = x.shape
    ffp = w_down.shape[0]
    nj = ffp // tf
    long_seq = T >= tm
    G, Tt = (1, tm) if long_seq else (tm // T, T)
    assert M % tm == 0 and T % Tt == 0 and (long_seq or Tt == BUF_ROWS)
    tiles_per_seq = T // Tt if long_seq else 1
    seq = (lambda i: i // tiles_per_seq) if long_seq else (lambda i: i)
    colspec = lambda rows: pl.BlockSpec((rows, tf), lambda i, j: (0, j))
    hspec = pl.BlockSpec((G, BUF_ROWS, tf), lambda i, j: (seq(i), 0, j))
    tspec = pl.BlockSpec((G, BUF_ROWS, tf), lambda i, j: (i, 0, j))
    tshape = jax.ShapeDtypeStruct((M // tm * G, BUF_ROWS, ffp), F32)
    y, tail_g, tail_v = pl.pallas_call(
        functools.partial(_ffn_kernel, G=G, Tt=Tt, nj=nj, tiles_per_seq=tiles_per_seq),
        grid=(M // tm, nj),
        in_specs=[pl.BlockSpec((tm, D), lambda i, j: (i, 0)),
                  colspec(D), colspec(D),
                  colspec(FFN_CONV), colspec(FFN_CONV),
                  colspec(1), colspec(1),
                  hspec, hspec,
                  pl.BlockSpec((tf, D), lambda i, j: (j, 0)),
                  pl.BlockSpec((1, D), lambda i, j: (0, 0)),
                  pl.BlockSpec((1, D), lambda i, j: (0, 0))],
        out_specs=[pl.BlockSpec((tm, D), lambda i, j: (i, 0)), tspec, tspec],
        out_shape=[jax.ShapeDtypeStruct((M, D), F32), tshape, tshape],
        scratch_shapes=[pltpu.VMEM((tm, D), BF16),
                        pltpu.VMEM((tm, D), F32),
                        pltpu.VMEM((2, G, BUF_ROWS + Tt, tf), F32),
                        pltpu.VMEM((nj, 2, BUF_ROWS, tf), F32)],
        compiler_params=_cparams("arbitrary", "arbitrary"),
        name="conv_ffn",
    )(x, w_up[0], w_up[1], conv_w[0], conv_w[1], conv_b[0].reshape(1, ffp), conv_b[1].reshape(1, ffp),
      hist[0], hist[1], w_down, g.reshape(1, D), b.reshape(1, D))
    last = slice(tiles_per_seq - 1, None, tiles_per_seq)
    return y, tail_g[last], tail_v[last]


def _lane_col(x, lane_iota, idx):
    return jnp.sum(jnp.where(lane_iota == idx, x, 0.0), axis=1, keepdims=True)


def _delta_kernel(q_ref, k_ref, v_ref, gate_ref, ab_ref, s0_ref, alog_ref, dtb_ref, nw_ref,
                  o_ref, sout_ref, s_scr, *, C, nchunks):
    n = pl.program_id(1)

    @pl.when(n == 0)
    def _():
        s_scr[...] = s0_ref[...]

    nb = q_ref.shape[0]
    H = nb * DN_HEADS
    gh = min(DN_HEADS, LANES // C)
    R = gh * C
    groups = [list(range(g * gh, (g + 1) * gh)) for g in range(H // gh)]
    seq = lambda h: h // DN_HEADS
    hsl = lambda h: slice((h % DN_HEADS) * HEAD_DIM, (h % DN_HEADS + 1) * HEAD_DIM)
    rsl = lambda j: slice(j * C, (j + 1) * C)
    cat0 = lambda xs: xs[0] if len(xs) == 1 else jnp.concatenate(xs, axis=0)

    row = lax.broadcasted_iota(jnp.int32, (R, R), 0)
    col = lax.broadcasted_iota(jnp.int32, (R, R), 1)
    same = (row // C) == (col // C)
    eye = row == col
    incl = same & (row >= col)
    strict = same & (row > col)
    eyef = jnp.where(eye, 1.0, 0.0).astype(F32)
    lane = lax.broadcasted_iota(jnp.int32, (C, LANES), 1)
    rowhead = lax.broadcasted_iota(jnp.int32, (R, HEAD_DIM), 0) // C

    trow = lax.broadcasted_iota(jnp.int32, (C, C), 0)
    tcol = lax.broadcasted_iota(jnp.int32, (C, C), 1)
    tril = jnp.where(trow >= tcol, 1.0, 0.0).astype(F32)
    gc_all, beta_all = [], []
    for s in range(nb):
        ab = ab_ref[s]
        z = ab + dtb_ref[...]
        softplus = jnp.maximum(z, 0.0) + jnp.log1p(jnp.exp(-jnp.abs(z)))
        g_all = -jnp.exp(alog_ref[...]) * softplus
        beta_all.append(_sigmoid(ab))
        if C >= 2 * SUBLANES:
            gc_all.append(_dot3(tril, g_all))
        else:
            gc_all.append(jnp.dot(tril, g_all, preferred_element_type=F32, precision=lax.Precision.HIGHEST))

    q = [cat0([q_ref[seq(h), :, hsl(h)] for h in hs]) for hs in groups]
    k = [cat0([k_ref[seq(h), :, hsl(h)] for h in hs]) for hs in groups]
    v = [cat0([v_ref[seq(h), :, hsl(h)] for h in hs]) for hs in groups]
    gcol_h = [_lane_col(gc_all[seq(h)], lane, h % DN_HEADS) for h in range(H)]
    gcol = [cat0([gcol_h[h] for h in hs]) for hs in groups]
    bcol = [cat0([_lane_col(beta_all[seq(h)], lane, DN_HEADS + h % DN_HEADS) for h in hs]) for hs in groups]
    glast_h = [g[C - 1:C, :] for g in gcol_h]
    glast = [cat0([jnp.broadcast_to(glast_h[h], (C, 1)) for h in hs]) for hs in groups]
    grow = [jnp.sum(jnp.where(eye, g, 0.0), axis=0, keepdims=True) for g in gcol]
    decay = [jnp.where(incl, jnp.exp(jnp.where(incl, gc - gr, 0.0)), 0.0) for gc, gr in zip(gcol, grow)]
    kb = [ki * bi for ki, bi in zip(k, bcol)]
    kq = [_dot_nt(jnp.concatenate([kbi, qi], axis=0).astype(BF16), ki.astype(BF16))
          for kbi, qi, ki in zip(kb, q, k)]
    m = [jnp.where(strict, x[:R] * d, 0.0) for x, d in zip(kq, decay)]
    attn = [x[R:] * d for x, d in zip(kq, decay)]

    nm = [-x for x in m]
    p = [eyef + x for x in nm]
    nk = [_dot3(x, x) for x in nm]
    terms = 2
    while terms * 2 < C:
        res = [_dot3(jnp.concatenate([pi, ni], axis=0), ni) for pi, ni in zip(p, nk)]
        p = [pi + r[:R] for pi, r in zip(p, res)]
        nk = [r[R:] for r in res]
        terms *= 2
    p = [pi + _dot3(pi, ni) for pi, ni in zip(p, nk)]

    eg = [jnp.exp(g) for g in gcol]
    uw = [_dot(pi.astype(BF16), jnp.concatenate([vi * bi, kbi * e], axis=1).astype(BF16))
          for pi, vi, bi, kbi, e in zip(p, v, bcol, kb, eg)]
    qg = [qi * e for qi, e in zip(q, eg)]
    kg = [ki * jnp.exp(gl - gc) for ki, gl, gc in zip(k, glast, gcol)]

    s_old = [s_scr[seq(h), h % DN_HEADS] for h in range(H)]
    sb = [s.astype(BF16) for s in s_old]
    ws = []
    for g, hs in enumerate(groups):
        for j, h in enumerate(hs):
            lhs = jnp.concatenate([uw[g][rsl(j), HEAD_DIM:], qg[g][rsl(j), :]], axis=0).astype(BF16)
            ws.append(_dot(lhs, sb[h]))
    v_new = [cat0([uw[g][rsl(j), :HEAD_DIM] - ws[h][:C] for j, h in enumerate(hs)])
             for g, hs in enumerate(groups)]
    vnb = [x.astype(BF16) for x in v_new]
    o_all = [cat0([ws[h][C:] for h in hs]) + _dot(a.astype(BF16), vb)
             for hs, a, vb in zip(groups, attn, vnb)]
    for g, hs in enumerate(groups):
        for j, h in enumerate(hs):
            kgm = jnp.where(rowhead == j, kg[g], 0.0).astype(BF16) if gh > 1 else kg[g].astype(BF16)
            s_scr[seq(h), h % DN_HEADS] = s_old[h] * jnp.exp(glast_h[h]) + _dot_tn(kgm, vnb[g])
    for g, hs in enumerate(groups):
        for j, h in enumerate(hs):
            o = o_all[g][rsl(j), :]
            o = o * lax.rsqrt(jnp.mean(o * o, axis=-1, keepdims=True) + NORM_EPS) * nw_ref[...]
            gate = gate_ref[seq(h), :, hsl(h)]
            o_ref[seq(h), :, hsl(h)] = (o * (gate * _sigmoid(gate))).astype(o_ref.dtype)

    @pl.when(n == nchunks - 1)
    def _():
        sout_ref[...] = s_scr[...]


def _delta(qkvn, proj, ab, s0, a_log, dt_bias, norm_w, *, C, nb):
    B, T, _ = qkvn.shape
    nchunks = T // C
    W = DN_HEADS * HEAD_DIM
    pad = LANES - DN_HEADS
    alog_p = jnp.pad(a_log.astype(F32), (0, pad)).reshape(1, LANES)
    dtb_p = jnp.pad(dt_bias.astype(F32), (0, pad)).reshape(1, LANES)
    tok = lambda col, width: pl.BlockSpec((nb, C, width), lambda b, n: (b, n, col))
    state = pl.BlockSpec((nb, DN_HEADS, HEAD_DIM, HEAD_DIM), lambda b, n: (b, 0, 0, 0))
    return pl.pallas_call(
        functools.partial(_delta_kernel, C=C, nchunks=nchunks),
        grid=(B // nb, nchunks),
        in_specs=[tok(0, W), tok(1, W), tok(2, W), tok(3, W), tok(0, LANES), state,
                  pl.BlockSpec((1, LANES), lambda b, n: (0, 0)),
                  pl.BlockSpec((1, LANES), lambda b, n: (0, 0)),
                  pl.BlockSpec((1, HEAD_DIM), lambda b, n: (0, 0))],
        out_specs=[tok(0, W), state],
        out_shape=[jax.ShapeDtypeStruct((B, T, W), BF16),
                   jax.ShapeDtypeStruct((B, DN_HEADS, HEAD_DIM, HEAD_DIM), F32)],
        scratch_shapes=[pltpu.VMEM((nb, DN_HEADS, HEAD_DIM, HEAD_DIM), F32)],
        compiler_params=_cparams("parallel", "arbitrary"),
        name="delta_rule",
    )(qkvn, qkvn, qkvn, proj, ab, s0, alog_p, dtb_p, norm_w.reshape(1, HEAD_DIM))


def _heads_to_lanes(ref, nheads, ntok):
    return jnp.concatenate(
        [ref[pl.ds(h, ntok, stride=nheads), :].astype(BF16) for h in range(nheads)], axis=1)


def _sb_blocks(qb, kblks, vblks, bias, carry, masks, ustrict):
    zs = [_dot_nt(qb, kb) * (ATT_SCALE * LOG2E) + bias * LOG2E for kb in kblks]
    nzs = [-z for z in zs]
    lks = [jnp.minimum(nz, 0.0) - jnp.log2(1.0 + jnp.exp2(jnp.minimum(z, nz))) for z, nz in zip(zs, nzs)]
    lss = [z + lk for z, lk in zip(zs, lks)]
    lks = [lk if mk is None else jnp.where(mk, lk, 0.0) for lk, mk in zip(lks, masks)]
    ustack = jnp.concatenate([ustrict, ustrict], axis=0)
    rs = [_dot(jnp.concatenate(_split2(lk), axis=1), ustack) for lk in lks]
    sums = [jnp.sum(lk, axis=1, keepdims=True) for lk in lks]
    acc = None
    for ls, r, sm, mk, vb in zip(lss, rs, sums, masks, vblks):
        a = jnp.exp2(ls + r + carry)
        if mk is not None:
            a = jnp.where(mk, a, 0.0)
        pv = _dot(a.astype(BF16), vb)
        acc = pv if acc is None else acc + pv
        carry = carry + sm
    return acc, carry


def _sbp_kernel(bias_ref, q_ref, k_ref, v_ref, o_ref, kb_scr, vb_scr, *, tq):
    h = pl.program_id(1)
    qi = pl.program_id(2)

    @pl.when(qi == 0)
    def _():
        kb_scr[...] = k_ref[...].astype(BF16)
        vb_scr[...] = v_ref[...].astype(BF16)

    qb = q_ref[...].astype(BF16)
    bias = bias_ref[h]
    row = lax.broadcasted_iota(jnp.int32, (tq, tq), 0)
    col = lax.broadcasted_iota(jnp.int32, (tq, tq), 1)
    ustrict = jnp.where(row > col, 1.0, 0.0).astype(BF16)

    def blocks(kjs, carry, masks):
        starts = [kj * tq if isinstance(kj, int) else pl.multiple_of(kj * tq, tq) for kj in kjs]
        return _sb_blocks(qb, [kb_scr[pl.ds(s, tq), :] for s in starts],
                          [vb_scr[pl.ds(s, tq), :] for s in starts], bias, carry, masks, ustrict)

    def group(n, first_kj, first_mask):
        def run(st):
            acc, carry = st
            pv, carry = blocks([first_kj - d for d in range(n)], carry, [first_mask] + [None] * (n - 1))
            return acc + pv, carry
        return run

    rest = qi % 4
    head = [group(n, qi, col < row) for n in (1, 2, 3, 4)]
    init = (jnp.zeros((tq, HEAD_DIM), F32), jnp.zeros((tq, 1), F32))
    st = lax.cond(rest < 2,
                  lambda s: lax.cond(rest == 0, head[0], head[1], s),
                  lambda s: lax.cond(rest == 2, head[2], head[3], s), init)
    acc, _ = lax.fori_loop(0, qi // 4, lambda it, s: group(4, qi - rest - 1 - 4 * it, None)(s), st)
    o_ref[...] = acc.astype(o_ref.dtype)


def _sb_prompt(proj, sb_bias, *, B, T, qcol, kcol, vcol, tq):
    nq = T // tq
    grid_spec = pltpu.PrefetchScalarGridSpec(
        num_scalar_prefetch=1,
        grid=(B, SB_HEADS, nq),
        in_specs=[pl.BlockSpec((tq, HEAD_DIM), lambda b, h, i, s: (b * nq + i, qcol + h)),
                  pl.BlockSpec((T, HEAD_DIM), lambda b, h, i, s: (b, kcol + h)),
                  pl.BlockSpec((T, HEAD_DIM), lambda b, h, i, s: (b, vcol + h))],
        out_specs=pl.BlockSpec((tq, HEAD_DIM), lambda b, h, i, s: (b * nq + i, h)),
        scratch_shapes=[pltpu.VMEM((T, HEAD_DIM), BF16), pltpu.VMEM((T, HEAD_DIM), BF16)],
    )
    return pl.pallas_call(
        functools.partial(_sbp_kernel, tq=tq),
        grid_spec=grid_spec,
        out_shape=jax.ShapeDtypeStruct((B * T, SB_HEADS * HEAD_DIM), BF16),
        compiler_params=_cparams("parallel", "parallel", "arbitrary"),
        name="sb_prompt",
    )(sb_bias.astype(F32), proj, proj, proj)


def _sbs_kernel(pt_ref, q_ref, kn_ref, vn_ref, *rest, T, nsteps, pps):
    kc_refs, vc_refs = rest[:pps], rest[pps:2 * pps]
    bias_ref, o_ref, qbd_scr, knp_scr, vnp_scr, acc_scr, carry_scr = rest[2 * pps:]
    p = pl.program_id(1)
    R = SB_HEADS * T
    W = SB_HEADS * HEAD_DIM
    rrow = lax.broadcasted_iota(jnp.int32, (R, W), 0)
    rcol = lax.broadcasted_iota(jnp.int32, (R, W), 1)
    own_head = (rrow // T) == (rcol // HEAD_DIM)
    row = lax.broadcasted_iota(jnp.int32, (PAGE_SIZE, PAGE_SIZE), 0)
    col = lax.broadcasted_iota(jnp.int32, (PAGE_SIZE, PAGE_SIZE), 1)
    ustrict = jnp.where(row > col, 1.0, 0.0).astype(BF16)
    bias = bias_ref[...]

    def step(kblks, vblks, masks):
        pv, carry = _sb_blocks(qbd_scr[...], kblks, vblks, bias, carry_scr[...], masks, ustrict)
        acc_scr[...] += pv
        carry_scr[...] = carry

    @pl.when(p == 0)
    def _():
        q = q_ref[...]
        qrep = jnp.concatenate([q] * SB_HEADS, axis=0)
        qbd_scr[...] = jnp.where(own_head, qrep, 0.0).astype(BF16)
        knp_scr[...] = jnp.zeros_like(knp_scr)
        vnp_scr[...] = jnp.zeros_like(vnp_scr)
        knp_scr[0:T, :] = kn_ref[...]
        vnp_scr[0:T, :] = vn_ref[...]
        acc_scr[...] = jnp.zeros_like(acc_scr)
        carry_scr[...] = jnp.zeros_like(carry_scr)
        qpos = lax.broadcasted_iota(jnp.int32, (R, PAGE_SIZE), 0) % T
        kpos = lax.broadcasted_iota(jnp.int32, (R, PAGE_SIZE), 1)
        step([knp_scr[...].astype(BF16)], [vnp_scr[...].astype(BF16)], [kpos < qpos])

    step([_heads_to_lanes(r.at[0], SB_HEADS, PAGE_SIZE) for r in kc_refs],
         [_heads_to_lanes(r.at[0], SB_HEADS, PAGE_SIZE) for r in vc_refs], [None] * pps)

    @pl.when(p == nsteps - 1)
    def _():
        accm = jnp.where(own_head, acc_scr[...], 0.0)
        out = accm[0:T]
        for h in range(1, SB_HEADS):
            out = out + accm[h * T:(h + 1) * T]
        o_ref[...] = out.astype(o_ref.dtype)


def _sb_sample(proj, cache_k, cache_v, page_table, sb_bias, *, B, T, qcol, kcol, vcol):
    npages = page_table.shape[1]
    pps = SB_PAGES_PER_STEP if npages % SB_PAGES_PER_STEP == 0 else 1
    nsteps = npages // pps
    W = SB_HEADS * HEAD_DIM
    R = SB_HEADS * T
    bias_rows = jnp.broadcast_to(jnp.repeat(sb_bias.astype(F32), T)[:, None], (R, PAGE_SIZE))

    def page_spec(i):
        return pl.BlockSpec((1, PAGE_SIZE * SB_HEADS, HEAD_DIM),
                            lambda b, p, pt: (pt[b, npages - 1 - (p * pps + i)], 0, 0))

    grid_spec = pltpu.PrefetchScalarGridSpec(
        num_scalar_prefetch=1,
        grid=(B, nsteps),
        in_specs=[pl.BlockSpec((T, W), lambda b, p, pt: (b, qcol)),
                  pl.BlockSpec((T, W), lambda b, p, pt: (b, kcol)),
                  pl.BlockSpec((T, W), lambda b, p, pt: (b, vcol))]
                 + [page_spec(i) for i in range(pps)] * 2
                 + [pl.BlockSpec((R, PAGE_SIZE), lambda b, p, pt: (0, 0))],
        out_specs=pl.BlockSpec((T, W), lambda b, p, pt: (b, 0)),
        scratch_shapes=[pltpu.VMEM((R, W), BF16),
                        pltpu.VMEM((PAGE_SIZE, W), F32),
                        pltpu.VMEM((PAGE_SIZE, W), F32),
                        pltpu.VMEM((R, W), F32),
                        pltpu.VMEM((R, 1), F32)],
    )
    return pl.pallas_call(
        functools.partial(_sbs_kernel, T=T, nsteps=nsteps, pps=pps),
        grid_spec=grid_spec,
        out_shape=jax.ShapeDtypeStruct((B * T, W), BF16),
        compiler_params=_cparams("arbitrary", "arbitrary"),
        name="sb_sample",
    )(page_table, proj, proj, proj, *([cache_k] * pps), *([cache_v] * pps), bias_rows)


def _xattn_prompt_kernel(q_ref, k_ref, v_ref, o_ref):
    for h in range(X_HEADS):
        sl = slice(h * HEAD_DIM, (h + 1) * HEAD_DIM)
        s = _dot_nt(q_ref[:, sl].astype(BF16), k_ref[0, :, sl].astype(BF16)) * ATT_SCALE
        p = jnp.exp(s - jnp.max(s, axis=-1, keepdims=True))
        p = p / jnp.sum(p, axis=-1, keepdims=True)
        o_ref[:, sl] = _dot(p.astype(BF16), v_ref[0, :, sl].astype(BF16)).astype(o_ref.dtype)


def _xattn_prompt(q, mk, mv, *, B, T, tq):
    W = X_HEADS * HEAD_DIM
    nq = T // tq
    nm = mk.shape[1]
    return pl.pallas_call(
        _xattn_prompt_kernel,
        grid=(B, nq),
        in_specs=[pl.BlockSpec((tq, W), lambda b, i: (b * nq + i, 0)),
                  pl.BlockSpec((1, nm, W), lambda b, i: (b, 0, 0)),
                  pl.BlockSpec((1, nm, W), lambda b, i: (b, 0, 0))],
        out_specs=pl.BlockSpec((tq, W), lambda b, i: (b * nq + i, 0)),
        out_shape=jax.ShapeDtypeStruct((B * T, W), BF16),
        compiler_params=_cparams("parallel", "parallel"),
        name="xattn_prompt",
    )(q, mk, mv)


def _xattn_sample_kernel(q_ref, k_ref, v_ref, o_ref, *, G, T, nm):
    R = X_HEADS * T
    W = X_HEADS * HEAD_DIM
    rrow = lax.broadcasted_iota(jnp.int32, (R, W), 0)
    rcol = lax.broadcasted_iota(jnp.int32, (R, W), 1)
    own_head = (rrow // T) == (rcol // HEAD_DIM)
    for g in range(G):
        q = q_ref[g]
        qbd = jnp.where(own_head, jnp.concatenate([q] * X_HEADS, axis=0), 0.0).astype(BF16)
        s = _dot_nt(qbd, _heads_to_lanes(k_ref.at[g], X_HEADS, nm)) * ATT_SCALE
        p = jnp.exp(s - jnp.max(s, axis=-1, keepdims=True))
        p = p / jnp.sum(p, axis=-1, keepdims=True)
        of = jnp.where(own_head, _dot(p.astype(BF16), _heads_to_lanes(v_ref.at[g], X_HEADS, nm)), 0.0)
        out = of[0:T]
        for h in range(1, X_HEADS):
            out = out + of[h * T:(h + 1) * T]
        o_ref[g] = out.astype(o_ref.dtype)


def _xattn_sample(q3, mk, mv, layer, *, G):
    B, T, W = q3.shape
    nm = mk.shape[2] // X_HEADS
    return pl.pallas_call(
        functools.partial(_xattn_sample_kernel, G=G, T=T, nm=nm),
        grid=(B // G,),
        in_specs=[pl.BlockSpec((G, T, W), lambda b: (b, 0, 0)),
                  pl.BlockSpec((None, G, nm * X_HEADS, HEAD_DIM), lambda b: (layer, b, 0, 0)),
                  pl.BlockSpec((None, G, nm * X_HEADS, HEAD_DIM), lambda b: (layer, b, 0, 0))],
        out_specs=pl.BlockSpec((G, T, W), lambda b: (b, 0, 0)),
        out_shape=jax.ShapeDtypeStruct((B, T, W), BF16),
        compiler_params=_cparams("parallel"),
        name="xattn_sample",
    )(q3, mk, mv)


def _lru_kernel(y_ref, xc_ref, wr_ref, wi_ref, br_ref, bi_ref, lam_ref, h0_ref, o_ref, hl_ref,
                hc_scr, a_scr, b_scr, h_scr, *, G, Tt, nt):
    t = pl.program_id(1)
    rows = G * Tt
    ngroups = rows // SUBLANES
    tc = a_scr.shape[1]
    wb = tc // LRU_BLOCKS

    @pl.when(t == 0)
    def _():
        hc_scr[...] = h0_ref[...]

    xc = xc_ref[...].reshape(rows, tc)
    xbs = [xc[:, n * wb:(n + 1) * wb].astype(BF16) for n in range(LRU_BLOCKS)]
    gate = lambda w_ref, b_ref: _sigmoid(
        jnp.concatenate([_dot(xbs[n], w_ref[n]) for n in range(LRU_BLOCKS)], axis=1) + b_ref[...])
    r = gate(wr_ref, br_ref)
    i = gate(wi_ref, bi_ref)
    log_a = LRU_C * r * _neg_softplus(-lam_ref[...])
    a = jnp.exp(log_a)
    a_scr[...] = a
    b_scr[...] = jnp.sqrt(-jnp.tanh(log_a) * (a * a + 1.0)) * (i * xc)
    sub = lax.broadcasted_iota(jnp.int32, (SUBLANES, tc), 0)

    def group(gi, carry):
        r0 = pl.multiple_of(gi * SUBLANES, SUBLANES)
        a = a_scr[pl.ds(r0, SUBLANES), :]
        b = b_scr[pl.ds(r0, SUBLANES), :]
        for s in (1, 2, 4):
            keep = sub >= s
            a_sh = jnp.where(keep, pltpu.roll(a, s, 0), 1.0)
            b_sh = jnp.where(keep, pltpu.roll(b, s, 0), 0.0)
            b = a * b_sh + b
            a = a * a_sh
        hprev = hc_scr[gi] if Tt == SUBLANES else carry
        h8 = b + a * hprev
        h_scr[pl.ds(r0, SUBLANES), :] = h8
        last = h8[SUBLANES - 1:SUBLANES, :]
        if Tt == SUBLANES:
            hl_ref[gi] = last
            return carry
        return last

    carry = lax.fori_loop(0, ngroups, group, hc_scr[0])
    if Tt != SUBLANES:
        hc_scr[0] = carry

        @pl.when(t == nt - 1)
        def _():
            hl_ref[0] = carry

    y = y_ref[...].reshape(rows, tc)
    o_ref[...] = (jax.nn.gelu(y) * h_scr[...]).astype(o_ref.dtype).reshape(G, Tt, tc)


def _lru(proj3, xc3, w_r, w_i, b_r, b_i, lam, h0, *, G, Tt):
    B, T, D = xc3.shape
    wb = D // LRU_BLOCKS
    nt = T // Tt
    assert Tt % SUBLANES == 0 and (G == 1 or Tt == SUBLANES)
    vec = lambda a: a.astype(F32).reshape(1, D)
    whole = lambda shape: pl.BlockSpec(shape, lambda b, t: (0,) * len(shape))
    return pl.pallas_call(
        functools.partial(_lru_kernel, G=G, Tt=Tt, nt=nt),
        grid=(B // G, nt),
        in_specs=[pl.BlockSpec((G, Tt, D), lambda b, t: (b, t, 0)),
                  pl.BlockSpec((G, Tt, D), lambda b, t: (b, t, 0)),
                  whole((LRU_BLOCKS, wb, wb)), whole((LRU_BLOCKS, wb, wb)),
                  whole((1, D)), whole((1, D)), whole((1, D)),
                  pl.BlockSpec((G, 1, D), lambda b, t: (b, 0, 0))],
        out_specs=[pl.BlockSpec((G, Tt, D), lambda b, t: (b, t, 0)),
                   pl.BlockSpec((G, 1, D), lambda b, t: (b, 0, 0))],
        out_shape=[jax.ShapeDtypeStruct((B, T, D), BF16),
                   jax.ShapeDtypeStruct((B, 1, D), F32)],
        scratch_shapes=[pltpu.VMEM((G, 1, D), F32),
                        pltpu.VMEM((G * Tt, D), F32),
                        pltpu.VMEM((G * Tt, D), F32),
                        pltpu.VMEM((G * Tt, D), F32)],
        compiler_params=_cparams("parallel", "arbitrary"),
        name="rglru",
    )(proj3, xc3, w_r, w_i, vec(b_r), vec(b_i), vec(lam), h0.reshape(B, 1, D))


def _pad_buf(buf):
    return jnp.pad(buf, ((0, 0), (BUF_ROWS - buf.shape[1], 0), (0, 0)))


def _row_tile(m):
    return 512 if m % 512 == 0 else m


def _mm_row_tile(m):
    return 1024 if m % 1024 == 0 else _row_tile(m)


def _col_tile(n):
    return 1024 if n % 1024 == 0 else 512 if n % 512 == 0 else n


def _wblocks(w):
    return w.astype(BF16)


class _Stream:
    def __init__(self, B, T):
        self.B, self.T = B, T
        self.M = B * T
        self.long = T > SUBLANES
        self.G = 1 if self.long else B


def _ab_layer(st, x, w, dn_conv, dn_S, sb_fn):
    B, T, M = st.B, st.T, st.M
    tm = _mm_row_tile(M)
    proj = _mm(x, w["ab_main"], tm=tm)
    ab = _mm(x, w["ab_ab"], tm=tm)
    proj3 = proj.reshape(B, T, -1)
    nqkv = 3 * DN_HEADS * HEAD_DIM
    qkvn = _conv_dn(proj3, _pad_buf(dn_conv), w["ab_conv_w"], G=st.G)
    C = 64 if st.long else T
    o_dn, s_new = _delta(qkvn, proj3, ab.reshape(B, T, LANES), dn_S, w["dn_A_log"], w["dn_dt_bias"],
                         w["dn_norm_w"], C=C, nb=2 if B % 2 == 0 else 1)
    o_sb = sb_fn(proj)
    o = jnp.concatenate([o_dn.reshape(M, -1), o_sb], axis=-1)
    new_conv = proj3[:, T - (DN_CONV - 1):, :nqkv]
    W = SB_HEADS * HEAD_DIM
    k_new = proj[:, 5 * W:6 * W].reshape(B, T, SB_HEADS, HEAD_DIM)
    v_new = proj[:, 6 * W:7 * W].reshape(B, T, SB_HEADS, HEAD_DIM)
    return o, w["ab_w_out"], (k_new, v_new, new_conv, s_new)


def _lru_layer(st, x, w, lru_conv, lru_h):
    B, T, M = st.B, st.T, st.M
    D = x.shape[1]
    proj = _mm(x, w["c_w_in"], tm=_mm_row_tile(M))
    proj3 = proj.reshape(B, T, 2 * D)
    tc = 256
    xc3 = _conv_lin(proj3, D // tc, _pad_buf(lru_conv), w["lru_conv_w"], w["lru_conv_b"], G=st.G, tc=tc)
    G, Tt = (1, 256) if st.long else (min(32, B), T)
    o3, h_last = _lru(proj3, xc3, w["lru_w_r"], w["lru_w_i"], w["lru_b_r"], w["lru_b_i"],
                      w["lru_lambda"], lru_h, G=G, Tt=Tt)
    new_conv = proj3[:, T - (LRU_CONV - 1):, D:]
    return o3.reshape(M, D), w["c_w_out"], (new_conv, h_last.reshape(B, D))


def _xattn_layer(st, x, w, mk3, mv3, layer):
    B, T, M = st.B, st.T, st.M
    q = _mm(x, w["xa_w_q"], tm=_mm_row_tile(M))
    if st.long:
        return _xattn_prompt(q, mk3, mv3, B=B, T=T, tq=512)
    W = q.shape[1]
    return _xattn_sample(q.reshape(B, T, W), mk3, mv3, layer, G=8).reshape(M, W)


def _ffn_layer(st, x, w, ffn_conv, d_ff, g, b):
    B, T, M = st.B, st.T, st.M
    ffp = w["ffn_w_down"].shape[0]
    halves = lambda a: (a[..., :d_ff], a[..., d_ff:])
    padc = lambda a: jnp.pad(a, [(0, 0)] * (a.ndim - 1) + [(0, ffp - d_ff)])
    hist = tuple(_pad_buf(padc(h)) for h in halves(ffn_conv))
    y, tail_g, tail_v = _ffn(x, w["ffn_w_up"], w["ffn_conv_w"], w["ffn_conv_b"], hist,
                             w["ffn_w_down"], g, b, B=B, T=T, tm=_row_tile(M), tf=FFN_COL_TILE)
    keep = slice(BUF_ROWS - (FFN_CONV - 1), BUF_ROWS)
    new_conv = jnp.concatenate([tail_g[:, keep, :d_ff], tail_v[:, keep, :d_ff]], axis=-1)
    return y, new_conv


def kernel(x_prompt, x_sample, mem_prompt, cache_sb_k, cache_sb_v, cache_mem_k, cache_mem_v, state_dn_conv, state_dn_S, state_lru_conv, state_lru_h, state_ffn_conv, page_table, ab_w_in, ab_conv_w, dn_A_log, dn_dt_bias, dn_norm_w, sb_bias, ab_w_out, c_w_in, lru_conv_w, lru_conv_b, lru_w_r, lru_b_r, lru_w_i, lru_b_i, lru_lambda, c_w_out, xa_w_q, xa_w_k, xa_w_v, xa_w_o, ffn_w_up, ffn_conv_w, ffn_conv_b, ffn_w_down, ln_g, ln_b):
    bp, tp, d_model = x_prompt.shape
    bs, ts, _ = x_sample.shape
    sp, ss = _Stream(bp, tp), _Stream(bs, ts)
    dt = x_prompt.dtype
    n_mem = mem_prompt.shape[1]
    xw = X_HEADS * HEAD_DIM
    nqkv = 3 * DN_HEADS * HEAD_DIM
    d_ff = ffn_w_down.shape[1]
    ffp = -(-d_ff // FFN_COL_TILE) * FFN_COL_TILE
    depth = ln_g.shape[0]

    xp = x_prompt.reshape(sp.M, d_model)
    xs = x_sample.reshape(ss.M, d_model)
    mem2 = mem_prompt.reshape(bp * n_mem, d_model)

    outs_p = {k: [] for k in ("sb_k", "sb_v", "dn_conv", "dn_S", "lru_conv", "lru_h", "mem_k", "mem_v", "ffn_conv")}
    outs_s = {k: [] for k in ("sb_k", "sb_v", "dn_conv", "dn_S", "lru_conv", "lru_h", "ffn_conv")}

    def pad_ff(a, axis):
        pad = [(0, 0)] * a.ndim
        pad[axis] = (0, ffp - d_ff)
        return jnp.pad(a, pad)

    for layer in range(depth):
        w = {}
        if layer % 2 == 0:
            e = layer // 2
            w_in = ab_w_in[e]
            c_ab = nqkv + DN_HEADS * HEAD_DIM
            w["ab_main"] = _wblocks(jnp.concatenate([w_in[:, :c_ab], w_in[:, c_ab + 2 * DN_HEADS:]], axis=1))
            w["ab_ab"] = _wblocks(jnp.pad(w_in[:, c_ab:c_ab + 2 * DN_HEADS], ((0, 0), (0, LANES - 2 * DN_HEADS))))
            w["ab_conv_w"] = ab_conv_w[e]
            w["dn_A_log"], w["dn_dt_bias"], w["dn_norm_w"] = dn_A_log[e], dn_dt_bias[e], dn_norm_w[e]
            w["ab_w_out"] = ab_w_out[e].astype(BF16)
        else:
            o = layer // 2
            w["c_w_in"] = _wblocks(c_w_in[o])
            w["lru_conv_w"], w["lru_conv_b"] = lru_conv_w[o], lru_conv_b[o]
            w["lru_w_r"], w["lru_w_i"] = lru_w_r[o].astype(BF16), lru_w_i[o].astype(BF16)
            w["lru_b_r"], w["lru_b_i"], w["lru_lambda"] = lru_b_r[o], lru_b_i[o], lru_lambda[o]
            w["c_w_out"] = c_w_out[o].astype(BF16)
        w["xa_w_q"] = _wblocks(xa_w_q[layer])
        w["ffn_w_up"] = (pad_ff(ffn_w_up[layer][:, :d_ff].astype(BF16), 1),
                         pad_ff(ffn_w_up[layer][:, d_ff:].astype(BF16), 1))
        w["ffn_conv_w"] = (pad_ff(ffn_conv_w[layer][:, :d_ff], 1), pad_ff(ffn_conv_w[layer][:, d_ff:], 1))
        w["ffn_conv_b"] = (pad_ff(ffn_conv_b[layer][:d_ff], 0), pad_ff(ffn_conv_b[layer][d_ff:], 0))
        w["ffn_w_down"] = pad_ff(ffn_w_down[layer].astype(BF16), 0)
        w_xo = xa_w_o[layer].astype(BF16)

        if layer % 2 == 0:
            e = layer // 2
            qcol, kcol, vcol = 4 * DN_HEADS, 5 * DN_HEADS, 6 * DN_HEADS
            sbp = functools.partial(_sb_prompt, sb_bias=sb_bias[e], B=bp, T=tp, qcol=qcol, kcol=kcol, vcol=vcol, tq=256)
            op, wout, (k_new, v_new, cb, s_new) = _ab_layer(
                sp, xp, w, jnp.zeros((bp, DN_CONV - 1, nqkv), dt),
                jnp.zeros((bp, DN_HEADS, HEAD_DIM, HEAD_DIM), F32), sbp)
            outs_p["sb_k"].append(k_new); outs_p["sb_v"].append(v_new)
            outs_p["dn_conv"].append(cb); outs_p["dn_S"].append(s_new)
            n_pool = cache_sb_k.shape[1]
            ck = cache_sb_k[e].reshape(n_pool, PAGE_SIZE * SB_HEADS, HEAD_DIM)
            cv = cache_sb_v[e].reshape(n_pool, PAGE_SIZE * SB_HEADS, HEAD_DIM)
            sbs = functools.partial(_sb_sample, cache_k=ck, cache_v=cv, page_table=page_table, sb_bias=sb_bias[e],
                                    B=bs, T=ts, qcol=4, kcol=5, vcol=6)
            os_, _, (k_new, v_new, cb, s_new) = _ab_layer(ss, xs, w, state_dn_conv[e], state_dn_S[e], sbs)
            outs_s["sb_k"].append(k_new); outs_s["sb_v"].append(v_new)
            outs_s["dn_conv"].append(cb); outs_s["dn_S"].append(s_new)
        else:
            o = layer // 2
            op, wout, (cb, hl) = _lru_layer(sp, xp, w, jnp.zeros((bp, LRU_CONV - 1, d_model), dt),
                                            jnp.zeros((bp, d_model), dt))
            outs_p["lru_conv"].append(cb); outs_p["lru_h"].append(hl)
            os_, _, (cb, hl) = _lru_layer(ss, xs, w, state_lru_conv[o], state_lru_h[o])
            outs_s["lru_conv"].append(cb); outs_s["lru_h"].append(hl)
        xp = _mm_ln(op, wout, xp, ln_g[layer, 0], ln_b[layer, 0], tm=LN_ROW_TILE // 2, tk=wout.shape[0])
        xs = _mm_ln(os_, wout, xs, ln_g[layer, 0], ln_b[layer, 0], tm=LN_ROW_TILE // 2, tk=wout.shape[0])

        mk = _mm(mem2, _wblocks(xa_w_k[layer]), tm=_row_tile(bp * n_mem))
        mv = _mm(mem2, _wblocks(xa_w_v[layer]), tm=_row_tile(bp * n_mem))
        outs_p["mem_k"].append(mk.reshape(bp, n_mem, X_HEADS, HEAD_DIM))
        outs_p["mem_v"].append(mv.reshape(bp, n_mem, X_HEADS, HEAD_DIM))
        ap = _xattn_layer(sp, xp, w, mk.reshape(bp, n_mem, xw), mv.reshape(bp, n_mem, xw), layer)
        as_ = _xattn_layer(ss, xs, w, cache_mem_k.reshape(depth, bs, n_mem * X_HEADS, HEAD_DIM),
                           cache_mem_v.reshape(depth, bs, n_mem * X_HEADS, HEAD_DIM), layer)
        xp = _mm_ln(ap, w_xo, xp, ln_g[layer, 1], ln_b[layer, 1], tm=LN_ROW_TILE, tk=xw)
        xs = _mm_ln(as_, w_xo, xs, ln_g[layer, 1], ln_b[layer, 1], tm=LN_ROW_TILE, tk=xw)

        xp, fb = _ffn_layer(sp, xp, w, jnp.zeros((bp, FFN_CONV - 1, 2 * d_ff), dt), d_ff,
                            ln_g[layer, 2], ln_b[layer, 2])
        outs_p["ffn_conv"].append(fb)
        xs, fb = _ffn_layer(ss, xs, w, state_ffn_conv[layer], d_ff, ln_g[layer, 2], ln_b[layer, 2])
        outs_s["ffn_conv"].append(fb)

    st = lambda xs_: jnp.stack(xs_)
    return (xp.reshape(bp, tp, d_model), xs.reshape(bs, ts, d_model),
            st(outs_p["sb_k"]), st(outs_p["sb_v"]), st(outs_p["dn_conv"]), st(outs_p["dn_S"]),
            st(outs_p["lru_conv"]), st(outs_p["lru_h"]), st(outs_p["mem_k"]), st(outs_p["mem_v"]),
            st(outs_p["ffn_conv"]),
            st(outs_s["sb_k"]), st(outs_s["sb_v"]), st(outs_s["dn_conv"]), st(outs_s["dn_S"]),
            st(outs_s["lru_conv"]), st(outs_s["lru_h"]), st(outs_s["ffn_conv"]))
```

```python
import functools

import jax
import jax.numpy as jnp
from jax import lax
from jax.experimental import pallas as pl
from jax.experimental.pallas import tpu as pltpu

F32 = jnp.float32
BF16 = jnp.bfloat16

HEAD_DIM = 128
DN_HEADS = 8
SB_HEADS = 8
X_HEADS = 4
DN_CONV = 4
LRU_CONV = 4
FFN_CONV = 3
LRU_BLOCKS = 8
LRU_C = 8.0
PAGE_SIZE = 128
DEPTH = 2
DEEPNORM_ALPHA = (2.0 * DEPTH) ** 0.25
LN_EPS = 1e-5
NORM_EPS = 1e-6
ATT_SCALE = HEAD_DIM ** -0.5
LOG2E = 1.4426950408889634

LANES = 128
SUBLANES = 8
VMEM_LIMIT_BYTES = 56 * 1024 * 1024
BUF_ROWS = SUBLANES
SB_PAGES_PER_STEP = 16
LN_ROW_TILE = 512
FFN_COL_TILE = 512
FFN_SUB_TILE = 256


def _cparams(*sem):
    return pltpu.CompilerParams(dimension_semantics=sem, vmem_limit_bytes=VMEM_LIMIT_BYTES)


def _dot(a, b):
    return jnp.dot(a, b, preferred_element_type=F32)


def _dot_nt(a, b):
    return lax.dot_general(a, b, (((1,), (1,)), ((), ())), preferred_element_type=F32)


def _dot_tn(a, b):
    return lax.dot_general(a, b, (((0,), (0,)), ((), ())), preferred_element_type=F32)


def _split2(x):
    hi = x.astype(BF16)
    lo = (x - hi.astype(F32)).astype(BF16)
    return hi, lo


def _dot3(a, b):
    ah, al = _split2(a)
    bh, bl = _split2(b)
    return _dot(ah, bh) + _dot(ah, bl) + _dot(al, bh)


def _sigmoid(x):
    return 1.0 / (1.0 + jnp.exp(-x))


def _neg_softplus(z):
    return jnp.minimum(-z, 0.0) - jnp.log1p(jnp.exp(-jnp.abs(z)))


def _mm_kernel(x_ref, w_ref, o_ref, xb_ref):
    @pl.when(pl.program_id(1) == 0)
    def _():
        xb_ref[...] = x_ref[...].astype(BF16)

    o_ref[...] = _dot(xb_ref[...], w_ref[...]).astype(o_ref.dtype)


def _mm(x, w, *, tm, out_dtype=F32):
    M, K = x.shape
    N = w.shape[1]
    tn = _col_tile(N)
    assert M % tm == 0 and w.shape[0] == K
    return pl.pallas_call(
        _mm_kernel,
        grid=(M // tm, N // tn),
        in_specs=[pl.BlockSpec((tm, K), lambda i, j: (i, 0)),
                  pl.BlockSpec((K, tn), lambda i, j: (0, j))],
        out_specs=pl.BlockSpec((tm, tn), lambda i, j: (i, j)),
        out_shape=jax.ShapeDtypeStruct((M, N), out_dtype),
        scratch_shapes=[pltpu.VMEM((tm, K), BF16)],
        compiler_params=_cparams("parallel", "arbitrary"),
        name="proj",
    )(x, w)


def _mm_ln_kernel(x_ref, w_ref, r_ref, g_ref, b_ref, o_ref, acc_ref, *, nk):
    k = pl.program_id(1)
    part = _dot(x_ref[...].astype(BF16), w_ref[...])

    @pl.when(k == 0)
    def _():
        acc_ref[...] = part

    @pl.when(k > 0)
    def _():
        acc_ref[...] += part

    @pl.when(k == nk - 1)
    def _():
        h = DEEPNORM_ALPHA * r_ref[...] + acc_ref[...]
        hc = h - jnp.mean(h, axis=-1, keepdims=True)
        var = jnp.mean(hc * hc, axis=-1, keepdims=True)
        o_ref[...] = hc * lax.rsqrt(var + LN_EPS) * g_ref[...] + b_ref[...]


def _mm_ln(x, w, resid, g, b, *, tm, tk):
    M, K = x.shape
    N = w.shape[1]
    tm = min(tm, M)
    assert M % tm == 0 and K % tk == 0
    nk = K // tk
    return pl.pallas_call(
        functools.partial(_mm_ln_kernel, nk=nk),
        grid=(M // tm, nk),
        in_specs=[pl.BlockSpec((tm, tk), lambda i, k: (i, k)),
                  pl.BlockSpec((tk, N), lambda i, k: (k, 0)),
                  pl.BlockSpec((tm, N), lambda i, k: (i, 0)),
                  pl.BlockSpec((1, N), lambda i, k: (0, 0)),
                  pl.BlockSpec((1, N), lambda i, k: (0, 0))],
        out_specs=pl.BlockSpec((tm, N), lambda i, k: (i, 0)),
        out_shape=jax.ShapeDtypeStruct((M, N), F32),
        scratch_shapes=[pltpu.VMEM((tm, N), F32)],
        compiler_params=_cparams("parallel", "arbitrary"),
        name="proj_ln",
    )(x, w, resid, g.reshape(1, N), b.reshape(1, N))


def _conv_taps(u_ref, buf_ref, w_ref, ext_ref, width):
    T = u_ref.shape[1]
    ext_ref[:, 0:BUF_ROWS, :] = buf_ref[...]
    ext_ref[:, BUF_ROWS:BUF_ROWS + T, :] = u_ref[...]
    y = None
    for j in range(width):
        off = BUF_ROWS - (width - 1) + j
        term = ext_ref[:, off:off + T, :] * w_ref[j:j + 1, :][None]
        y = term if y is None else y + term
    return y


def _conv_dn_kernel(u_ref, buf_ref, w_ref, o_ref, ext_ref):
    c = pl.program_id(1)
    y = _conv_taps(u_ref, buf_ref, w_ref, ext_ref, DN_CONV)
    y = y * _sigmoid(y)
    ss = jnp.sum(y * y, axis=-1, keepdims=True)
    scale = jnp.where(c < DN_HEADS, ATT_SCALE, 1.0).astype(F32)
    yn = y * lax.rsqrt(ss + NORM_EPS) * scale
    o_ref[...] = jnp.where(c < 2 * DN_HEADS, yn, y)


def _conv_dn(u3, bufpad, w, *, G):
    B, T, _ = u3.shape
    C = w.shape[1]
    nc = C // HEAD_DIM
    return pl.pallas_call(
        _conv_dn_kernel,
        grid=(B // G, nc),
        in_specs=[pl.BlockSpec((G, T, HEAD_DIM), lambda b, c: (b, 0, c)),
                  pl.BlockSpec((G, BUF_ROWS, HEAD_DIM), lambda b, c: (b, 0, c)),
                  pl.BlockSpec((DN_CONV, HEAD_DIM), lambda b, c: (0, c))],
        out_specs=pl.BlockSpec((G, T, HEAD_DIM), lambda b, c: (b, 0, c)),
        out_shape=jax.ShapeDtypeStruct((B, T, C), F32),
        scratch_shapes=[pltpu.VMEM((G, BUF_ROWS + T, HEAD_DIM), F32)],
        compiler_params=_cparams("parallel", "parallel"),
        name="conv_dn",
    )(u3, bufpad, w)


def _conv_lin_kernel(u_ref, buf_ref, w_ref, b_ref, o_ref, ext_ref):
    y = _conv_taps(u_ref, buf_ref, w_ref, ext_ref, LRU_CONV)
    o_ref[...] = y + b_ref[...][None]


def _conv_lin(u3, col0, bufpad, w, bias, *, G, tc):
    B, T, _ = u3.shape
    C = w.shape[1]
    return pl.pallas_call(
        _conv_lin_kernel,
        grid=(B // G, C // tc),
        in_specs=[pl.BlockSpec((G, T, tc), lambda b, c: (b, 0, col0 + c)),
                  pl.BlockSpec((G, BUF_ROWS, tc), lambda b, c: (b, 0, c)),
                  pl.BlockSpec((LRU_CONV, tc), lambda b, c: (0, c)),
                  pl.BlockSpec((1, tc), lambda b, c: (0, c))],
        out_specs=pl.BlockSpec((G, T, tc), lambda b, c: (b, 0, c)),
        out_shape=jax.ShapeDtypeStruct((B, T, C), F32),
        scratch_shapes=[pltpu.VMEM((G, BUF_ROWS + T, tc), F32)],
        compiler_params=_cparams("parallel", "parallel"),
        name="conv_lin",
    )(u3, bufpad, w, bias.reshape(1, C))


def _ffn_kernel(x_ref, wg_ref, wv_ref, cwg_ref, cwv_ref, cbg_ref, cbv_ref, hg_ref, hv_ref, wd_ref,
                g_ref, b_ref, y_ref, tg_ref, tv_ref, xb_scr, acc_scr, ext_scr, carry_scr,
                *, G, Tt, nj, tiles_per_seq):
    i = pl.program_id(0)
    j = pl.program_id(1)
    tm, tf = G * Tt, ext_scr.shape[3]
    sub = min(tf, FFN_SUB_TILE)

    @pl.when(j == 0)
    def _():
        xb_scr[...] = x_ref[...].astype(BF16)
        acc_scr[...] = jnp.zeros_like(acc_scr)

    if G == 1:
        @pl.when((i == 0) & (j == 0))
        def _():
            carry_scr[...] = jnp.zeros_like(carry_scr)

    xb = xb_scr[...]
    sides = ((wg_ref, cwg_ref, cbg_ref, hg_ref, tg_ref), (wv_ref, cwv_ref, cbv_ref, hv_ref, tv_ref))
    subs = [slice(c, c + sub) for c in range(0, tf, sub)]
    ups = [[_dot(xb, side[0][:, cs]) for side in sides] for cs in subs]
    first = (i % tiles_per_seq) == 0
    part = None
    for cs, us in zip(subs, ups):
        ys = []
        for s, (u, (_, cw_ref, cb_ref, h_ref, t_ref)) in enumerate(zip(us, sides)):
            if G == 1:
                hist = jnp.where(first, h_ref[0, :, cs], carry_scr[j, s, :, cs])
                tail = u[tm - BUF_ROWS:, :]
                carry_scr[j, s, :, cs] = tail
                t_ref[0, :, cs] = tail
                ext_scr[s, 0, 0:BUF_ROWS, cs] = hist
            else:
                t_ref[:, :, cs] = u.reshape(G, Tt, sub)
                ext_scr[s, :, 0:BUF_ROWS, cs] = h_ref[:, :, cs]
            ext_scr[s, :, BUF_ROWS:BUF_ROWS + Tt, cs] = u.reshape(G, Tt, sub)
            y = None
            for t in range(FFN_CONV):
                off = BUF_ROWS - (FFN_CONV - 1) + t
                term = ext_scr[s, :, off:off + Tt, cs] * cw_ref[t:t + 1, cs][None]
                y = term if y is None else y + term
            ys.append(y + cb_ref[:, cs][None])
        yg, yv = ys
        h = (yg * _sigmoid(yg) * yv).reshape(tm, sub).astype(BF16)
        p = _dot(h, wd_ref[cs, :])
        part = p if part is None else part + p
    acc_scr[...] += part

    @pl.when(j == nj - 1)
    def _():
        hsum = DEEPNORM_ALPHA * x_ref[...] + acc_scr[...]
        hc = hsum - jnp.mean(hsum, axis=-1, keepdims=True)
        var = jnp.mean(hc * hc, axis=-1, keepdims=True)
        y_ref[...] = hc * lax.rsqrt(var + LN_EPS) * g_ref[...] + b_ref[...]


def _ffn(x, w_up, conv_w, conv_b, hist, w_down, g, b, *, B, T, tm, tf):
    M, D = x.shape
    ffp = w_down.shape[0]
    nj = ffp // tf
    long_seq = T >= tm
    G, Tt = (1, tm) if long_seq else (tm // T, T)
    assert M % tm == 0 and T % Tt == 0 and (long_seq or Tt == BUF_ROWS)
    tiles_per_seq = T // Tt if long_seq else 1
    seq = (lambda i: i // tiles_per_seq) if long_seq else (lambda i: i)
    colspec = lambda rows: pl.BlockSpec((rows, tf), lambda i, j: (0, j))
    hspec = pl.BlockSpec((G, BUF_ROWS, tf), lambda i, j: (seq(i), 0, j))
    tspec = pl.BlockSpec((G, BUF_ROWS, tf), lambda i, j: (i, 0, j))
    tshape = jax.ShapeDtypeStruct((M // tm * G, BUF_ROWS, ffp), F32)
    y, tail_g, tail_v = pl.pallas_call(
        functools.partial(_ffn_kernel, G=G, Tt=Tt, nj=nj, tiles_per_seq=tiles_per_seq),
        grid=(M // tm, nj),
        in_specs=[pl.BlockSpec((tm, D), lambda i, j: (i, 0)),
                  colspec(D), colspec(D),
                  colspec(FFN_CONV), colspec(FFN_CONV),
                  colspec(1), colspec(1),
                  hspec, hspec,
                  pl.BlockSpec((tf, D), lambda i, j: (j, 0)),
                  pl.BlockSpec((1, D), lambda i, j: (0, 0)),
                  pl.BlockSpec((1, D), lambda i, j: (0, 0))],
        out_specs=[pl.BlockSpec((tm, D), lambda i, j: (i, 0)), tspec, tspec],
        out_shape=[jax.ShapeDtypeStruct((M, D), F32), tshape, tshape],
        scratch_shapes=[pltpu.VMEM((tm, D), BF16),
                        pltpu.VMEM((tm, D), F32),
                        pltpu.VMEM((2, G, BUF_ROWS + Tt, tf), F32),
                        pltpu.VMEM((nj, 2, BUF_ROWS, tf), F32)],
        compiler_params=_cparams("arbitrary", "arbitrary"),
        name="conv_ffn",
    )(x, w_up[0], w_up[1], conv_w[0], conv_w[1], conv_b[0].reshape(1, ffp), conv_b[1].reshape(1, ffp),
      hist[0], hist[1], w_down, g.reshape(1, D), b.reshape(1, D))
    last = slice(tiles_per_seq - 1, None, tiles_per_seq)
    return y, tail_g[last], tail_v[last]


def _lane_col(x, lane_iota, idx):
    return jnp.sum(jnp.where(lane_iota == idx, x, 0.0), axis=1, keepdims=True)


def _delta_kernel(q_ref, k_ref, v_ref, gate_ref, ab_ref, s0_ref, alog_ref, dtb_ref, nw_ref,
                  o_ref, sout_ref, s_scr, *, C, nchunks):
    n = pl.program_id(1)

    @pl.when(n == 0)
    def _():
        s_scr[...] = s0_ref[...]

    nb = q_ref.shape[0]
    H = nb * DN_HEADS
    gh = min(DN_HEADS, LANES // C)
    R = gh * C
    groups = [list(range(g * gh, (g + 1) * gh)) for g in range(H // gh)]
    seq = lambda h: h // DN_HEADS
    hsl = lambda h: slice((h % DN_HEADS) * HEAD_DIM, (h % DN_HEADS + 1) * HEAD_DIM)
    rsl = lambda j: slice(j * C, (j + 1) * C)
    cat0 = lambda xs: xs[0] if len(xs) == 1 else jnp.concatenate(xs, axis=0)

    row = lax.broadcasted_iota(jnp.int32, (R, R), 0)
    col = lax.broadcasted_iota(jnp.int32, (R, R), 1)
    same = (row // C) == (col // C)
    eye = row == col
    incl = same & (row >= col)
    strict = same & (row > col)
    eyef = jnp.where(eye, 1.0, 0.0).astype(F32)
    lane = lax.broadcasted_iota(jnp.int32, (C, LANES), 1)
    rowhead = lax.broadcasted_iota(jnp.int32, (R, HEAD_DIM), 0) // C

    trow = lax.broadcasted_iota(jnp.int32, (C, C), 0)
    tcol = lax.broadcasted_iota(jnp.int32, (C, C), 1)
    tril = jnp.where(trow >= tcol, 1.0, 0.0).astype(F32)
    gc_all, beta_all = [], []
    for s in range(nb):
        ab = ab_ref[s]
        z = ab + dtb_ref[...]
        softplus = jnp.maximum(z, 0.0) + jnp.log1p(jnp.exp(-jnp.abs(z)))
        g_all = -jnp.exp(alog_ref[...]) * softplus
        beta_all.append(_sigmoid(ab))
        if C >= 2 * SUBLANES:
            gc_all.append(_dot3(tril, g_all))
        else:
            gc_all.append(jnp.dot(tril, g_all, preferred_element_type=F32, precision=lax.Precision.HIGHEST))

    q = [cat0([q_ref[seq(h), :, hsl(h)] for h in hs]) for hs in groups]
    k = [cat0([k_ref[seq(h), :, hsl(h)] for h in hs]) for hs in groups]
    v = [cat0([v_ref[seq(h), :, hsl(h)] for h in hs]) for hs in groups]
    gcol_h = [_lane_col(gc_all[seq(h)], lane, h % DN_HEADS) for h in range(H)]
    gcol = [cat0([gcol_h[h] for h in hs]) for hs in groups]
    bcol = [cat0([_lane_col(beta_all[seq(h)], lane, DN_HEADS + h % DN_HEADS) for h in hs]) for hs in groups]
    glast_h = [g[C - 1:C, :] for g in gcol_h]
    glast = [cat0([jnp.broadcast_to(glast_h[h], (C, 1)) for h in hs]) for hs in groups]
    grow = [jnp.sum(jnp.where(eye, g, 0.0), axis=0, keepdims=True) for g in gcol]
    decay = [jnp.where(incl, jnp.exp(jnp.where(incl, gc - gr, 0.0)), 0.0) for gc, gr in zip(gcol, grow)]
    kb = [ki * bi for ki, bi in zip(k, bcol)]
    kq = [_dot_nt(jnp.concatenate([kbi, qi], axis=0).astype(BF16), ki.astype(BF16))
          for kbi, qi, ki in zip(kb, q, k)]
    m = [jnp.where(strict, x[:R] * d, 0.0) for x, d in zip(kq, decay)]
    attn = [x[R:] * d for x, d in zip(kq, decay)]

    nm = [-x for x in m]
    p = [eyef + x for x in nm]
    nk = [_dot3(x, x) for x in nm]
    terms = 2
    while terms * 2 < C:
        res = [_dot3(jnp.concatenate([pi, ni], axis=0), ni) for pi, ni in zip(p, nk)]
        p = [pi + r[:R] for pi, r in zip(p, res)]
        nk = [r[R:] for r in res]
        terms *= 2
    p = [pi + _dot3(pi, ni) for pi, ni in zip(p, nk)]

    eg = [jnp.exp(g) for g in gcol]
    uw = [_dot(pi.astype(BF16), jnp.concatenate([vi * bi, kbi * e], axis=1).astype(BF16))
          for pi, vi, bi, kbi, e in zip(p, v, bcol, kb, eg)]
    qg = [qi * e for qi, e in zip(q, eg)]
    kg = [ki * jnp.exp(gl - gc) for ki, gl, gc in zip(k, glast, gcol)]

    s_old = [s_scr[seq(h), h % DN_HEADS] for h in range(H)]
    sb = [s.astype(BF16) for s in s_old]
    ws = []
    for g, hs in enumerate(groups):
        for j, h in enumerate(hs):
            lhs = jnp.concatenate([uw[g][rsl(j), HEAD_DIM:], qg[g][rsl(j), :]], axis=0).astype(BF16)
            ws.append(_dot(lhs, sb[h]))
    v_new = [cat0([uw[g][rsl(j), :HEAD_DIM] - ws[h][:C] for j, h in enumerate(hs)])
             for g, hs in enumerate(groups)]
    vnb = [x.astype(BF16) for x in v_new]
    o_all = [cat0([ws[h][C:] for h in hs]) + _dot(a.astype(BF16), vb)
             for hs, a, vb in zip(groups, attn, vnb)]
    for g, hs in enumerate(groups):
        for j, h in enumerate(hs):
            kgm = jnp.where(rowhead == j, kg[g], 0.0).astype(BF16) if gh > 1 else kg[g].astype(BF16)
            s_scr[seq(h), h % DN_HEADS] = s_old[h] * jnp.exp(glast_h[h]) + _dot_tn(kgm, vnb[g])
    for g, hs in enumerate(groups):
        for j, h in enumerate(hs):
            o = o_all[g][rsl(j), :]
            o = o * lax.rsqrt(jnp.mean(o * o, axis=-1, keepdims=True) + NORM_EPS) * nw_ref[...]
            gate = gate_ref[seq(h), :, hsl(h)]
            o_ref[seq(h), :, hsl(h)] = (o * (gate * _sigmoid(gate))).astype(o_ref.dtype)

    @pl.when(n == nchunks - 1)
    def _():
        sout_ref[...] = s_scr[...]


def _delta(qkvn, proj, ab, s0, a_log, dt_bias, norm_w, *, C, nb):
    B, T, _ = qkvn.shape
    nchunks = T // C
    W = DN_HEADS * HEAD_DIM
    pad = LANES - DN_HEADS
    alog_p = jnp.pad(a_log.astype(F32), (0, pad)).reshape(1, LANES)
    dtb_p = jnp.pad(dt_bias.astype(F32), (0, pad)).reshape(1, LANES)
    tok = lambda col, width: pl.BlockSpec((nb, C, width), lambda b, n: (b, n, col))
    state = pl.BlockSpec((nb, DN_HEADS, HEAD_DIM, HEAD_DIM), lambda b, n: (b, 0, 0, 0))
    return pl.pallas_call(
        functools.partial(_delta_kernel, C=C, nchunks=nchunks),
        grid=(B // nb, nchunks),
        in_specs=[tok(0, W), tok(1, W), tok(2, W), tok(3, W), tok(0, LANES), state,
                  pl.BlockSpec((1, LANES), lambda b, n: (0, 0)),
                  pl.BlockSpec((1, LANES), lambda b, n: (0, 0)),
                  pl.BlockSpec((1, HEAD_DIM), lambda b, n: (0, 0))],
        out_specs=[tok(0, W), state],
        out_shape=[jax.ShapeDtypeStruct((B, T, W), BF16),
                   jax.ShapeDtypeStruct((B, DN_HEADS, HEAD_DIM, HEAD_DIM), F32)],
        scratch_shapes=[pltpu.VMEM((nb, DN_HEADS, HEAD_DIM, HEAD_DIM), F32)],
        compiler_params=_cparams("parallel", "arbitrary"),
        name="delta_rule",
    )(qkvn, qkvn, qkvn, proj, ab, s0, alog_p, dtb_p, norm_w.reshape(1, HEAD_DIM))


def _heads_to_lanes(ref, nheads, ntok):
    return jnp.concatenate(
        [ref[pl.ds(h, ntok, stride=nheads), :].astype(BF16) for h in range(nheads)], axis=1)


def _sb_blocks(qb, kblks, vblks, bias, carry, masks, ustrict):
    zs = [_dot_nt(qb, kb) * (ATT_SCALE * LOG2E) + bias * LOG2E for kb in kblks]
    nzs = [-z for z in zs]
    lks = [jnp.minimum(nz, 0.0) - jnp.log2(1.0 + jnp.exp2(jnp.minimum(z, nz))) for z, nz in zip(zs, nzs)]
    lss = [z + lk for z, lk in zip(zs, lks)]
    lks = [lk if mk is None else jnp.where(mk, lk, 0.0) for lk, mk in zip(lks, masks)]
    ustack = jnp.concatenate([ustrict, ustrict], axis=0)
    rs = [_dot(jnp.concatenate(_split2(lk), axis=1), ustack) for lk in lks]
    sums = [jnp.sum(lk, axis=1, keepdims=True) for lk in lks]
    acc = None
    for ls, r, sm, mk, vb in zip(lss, rs, sums, masks, vblks):
        a = jnp.exp2(ls + r + carry)
        if mk is not None:
            a = jnp.where(mk, a, 0.0)
        pv = _dot(a.astype(BF16), vb)
        acc = pv if acc is None else acc + pv
        carry = carry + sm
    return acc, carry


def _sbp_kernel(bias_ref, q_ref, k_ref, v_ref, o_ref, kb_scr, vb_scr, *, tq):
    h = pl.program_id(1)
    qi = pl.program_id(2)

    @pl.when(qi == 0)
    def _():
        kb_scr[...] = k_ref[...].astype(BF16)
        vb_scr[...] = v_ref[...].astype(BF16)

    qb = q_ref[...].astype(BF16)
    bias = bias_ref[h]
    row = lax.broadcasted_iota(jnp.int32, (tq, tq), 0)
    col = lax.broadcasted_iota(jnp.int32, (tq, tq), 1)
    ustrict = jnp.where(row > col, 1.0, 0.0).astype(BF16)

    def blocks(kjs, carry, masks):
        starts = [kj * tq if isinstance(kj, int) else pl.multiple_of(kj * tq, tq) for kj in kjs]
        return _sb_blocks(qb, [kb_scr[pl.ds(s, tq), :] for s in starts],
                          [vb_scr[pl.ds(s, tq), :] for s in starts], bias, carry, masks, ustrict)

    def group(n, first_kj, first_mask):
        def run(st):
            acc, carry = st
            pv, carry = blocks([first_kj - d for d in range(n)], carry, [first_mask] + [None] * (n - 1))
            return acc + pv, carry
        return run

    rest = qi % 4
    head = [group(n, qi, col < row) for n in (1, 2, 3, 4)]
    init = (jnp.zeros((tq, HEAD_DIM), F32), jnp.zeros((tq, 1), F32))
    st = lax.cond(rest < 2,
                  lambda s: lax.cond(rest == 0, head[0], head[1], s),
                  lambda s: lax.cond(rest == 2, head[2], head[3], s), init)
    acc, _ = lax.fori_loop(0, qi // 4, lambda it, s: group(4, qi - rest - 1 - 4 * it, None)(s), st)
    o_ref[...] = acc.astype(o_ref.dtype)


def _sb_prompt(proj, sb_bias, *, B, T, qcol, kcol, vcol, tq):
    nq = T // tq
    grid_spec = pltpu.PrefetchScalarGridSpec(
        num_scalar_prefetch=1,
        grid=(B, SB_HEADS, nq),
        in_specs=[pl.BlockSpec((tq, HEAD_DIM), lambda b, h, i, s: (b * nq + i, qcol + h)),
                  pl.BlockSpec((T, HEAD_DIM), lambda b, h, i, s: (b, kcol + h)),
                  pl.BlockSpec((T, HEAD_DIM), lambda b, h, i, s: (b, vcol + h))],
        out_specs=pl.BlockSpec((tq, HEAD_DIM), lambda b, h, i, s: (b * nq + i, h)),
        scratch_shapes=[pltpu.VMEM((T, HEAD_DIM), BF16), pltpu.VMEM((T, HEAD_DIM), BF16)],
    )
    return pl.pallas_call(
        functools.partial(_sbp_kernel, tq=tq),
        grid_spec=grid_spec,
        out_shape=jax.ShapeDtypeStruct((B * T, SB_HEADS * HEAD_DIM), BF16),
        compiler_params=_cparams("parallel", "parallel", "arbitrary"),
        name="sb_prompt",
    )(sb_bias.astype(F32), proj, proj, proj)


def _sbs_kernel(pt_ref, q_ref, kn_ref, vn_ref, *rest, T, nsteps, pps):
    kc_refs, vc_refs = rest[:pps], rest[pps:2 * pps]
    bias_ref, o_ref, qbd_scr, knp_scr, vnp_scr, acc_scr, carry_scr = rest[2 * pps:]
    p = pl.program_id(1)
    R = SB_HEADS * T
    W = SB_HEADS * HEAD_DIM
    rrow = lax.broadcasted_iota(jnp.int32, (R, W), 0)
    rcol = lax.broadcasted_iota(jnp.int32, (R, W), 1)
    own_head = (rrow // T) == (rcol // HEAD_DIM)
    row = lax.broadcasted_iota(jnp.int32, (PAGE_SIZE, PAGE_SIZE), 0)
    col = lax.broadcasted_iota(jnp.int32, (PAGE_SIZE, PAGE_SIZE), 1)
    ustrict = jnp.where(row > col, 1.0, 0.0).astype(BF16)
    bias = bias_ref[...]

    def step(kblks, vblks, masks):
        pv, carry = _sb_blocks(qbd_scr[...], kblks, vblks, bias, carry_scr[...], masks, ustrict)
        acc_scr[...] += pv
        carry_scr[...] = carry

    @pl.when(p == 0)
    def _():
        q = q_ref[...]
        qrep = jnp.concatenate([q] * SB_HEADS, axis=0)
        qbd_scr[...] = jnp.where(own_head, qrep, 0.0).astype(BF16)
        knp_scr[...] = jnp.zeros_like(knp_scr)
        vnp_scr[...] = jnp.zeros_like(vnp_scr)
        knp_scr[0:T, :] = kn_ref[...]
        vnp_scr[0:T, :] = vn_ref[...]
        acc_scr[...] = jnp.zeros_like(acc_scr)
        carry_scr[...] = jnp.zeros_like(carry_scr)
        qpos = lax.broadcasted_iota(jnp.int32, (R, PAGE_SIZE), 0) % T
        kpos = lax.broadcasted_iota(jnp.int32, (R, PAGE_SIZE), 1)
        step([knp_scr[...].astype(BF16)], [vnp_scr[...].astype(BF16)], [kpos < qpos])

    step([_heads_to_lanes(r.at[0], SB_HEADS, PAGE_SIZE) for r in kc_refs],
         [_heads_to_lanes(r.at[0], SB_HEADS, PAGE_SIZE) for r in vc_refs], [None] * pps)

    @pl.when(p == nsteps - 1)
    def _():
        accm = jnp.where(own_head, acc_scr[...], 0.0)
        out = accm[0:T]
        for h in range(1, SB_HEADS):
            out = out + accm[h * T:(h + 1) * T]
        o_ref[...] = out.astype(o_ref.dtype)


def _sb_sample(proj, cache_k, cache_v, page_table, sb_bias, *, B, T, qcol, kcol, vcol):
    npages = page_table.shape[1]
    pps = SB_PAGES_PER_STEP if npages % SB_PAGES_PER_STEP == 0 else 1
    nsteps = npages // pps
    W = SB_HEADS * HEAD_DIM
    R = SB_HEADS * T
    bias_rows = jnp.broadcast_to(jnp.repeat(sb_bias.astype(F32), T)[:, None], (R, PAGE_SIZE))

    def page_spec(i):
        return pl.BlockSpec((1, PAGE_SIZE * SB_HEADS, HEAD_DIM),
                            lambda b, p, pt: (pt[b, npages - 1 - (p * pps + i)], 0, 0))

    grid_spec = pltpu.PrefetchScalarGridSpec(
        num_scalar_prefetch=1,
        grid=(B, nsteps),
        in_specs=[pl.BlockSpec((T, W), lambda b, p, pt: (b, qcol)),
                  pl.BlockSpec((T, W), lambda b, p, pt: (b, kcol)),
                  pl.BlockSpec((T, W), lambda b, p, pt: (b, vcol))]
                 + [page_spec(i) for i in range(pps)] * 2
                 + [pl.BlockSpec((R, PAGE_SIZE), lambda b, p, pt: (0, 0))],
        out_specs=pl.BlockSpec((T, W), lambda b, p, pt: (b, 0)),
        scratch_shapes=[pltpu.VMEM((R, W), BF16),
                        pltpu.VMEM((PAGE_SIZE, W), F32),
                        pltpu.VMEM((PAGE_SIZE, W), F32),
                        pltpu.VMEM((R, W), F32),
                        pltpu.VMEM((R, 1), F32)],
    )
    return pl.pallas_call(
        functools.partial(_sbs_kernel, T=T, nsteps=nsteps, pps=pps),
        grid_spec=grid_spec,
        out_shape=jax.ShapeDtypeStruct((B * T, W), BF16),
        compiler_params=_cparams("arbitrary", "arbitrary"),
        name="sb_sample",
    )(page_table, proj, proj, proj, *([cache_k] * pps), *([cache_v] * pps), bias_rows)


def _xattn_prompt_kernel(q_ref, k_ref, v_ref, o_ref):
    for h in range(X_HEADS):
        sl = slice(h * HEAD_DIM, (h + 1) * HEAD_DIM)
        s = _dot_nt(q_ref[:, sl].astype(BF16), k_ref[0, :, sl].astype(BF16)) * ATT_SCALE
        p = jnp.exp(s - jnp.max(s, axis=-1, keepdims=True))
        p = p / jnp.sum(p, axis=-1, keepdims=True)
        o_ref[:, sl] = _dot(p.astype(BF16), v_ref[0, :, sl].astype(BF16)).astype(o_ref.dtype)


def _xattn_prompt(q, mk, mv, *, B, T, tq):
    W = X_HEADS * HEAD_DIM
    nq = T // tq
    nm = mk.shape[1]
    return pl.pallas_call(
        _xattn_prompt_kernel,
        grid=(B, nq),
        in_specs=[pl.BlockSpec((tq, W), lambda b, i: (b * nq + i, 0)),
                  pl.BlockSpec((1, nm, W), lambda b, i: (b, 0, 0)),
                  pl.BlockSpec((1, nm, W), lambda b, i: (b, 0, 0))],
        out_specs=pl.BlockSpec((tq, W), lambda b, i: (b * nq + i, 0)),
        out_shape=jax.ShapeDtypeStruct((B * T, W), BF16),
        compiler_params=_cparams("parallel", "parallel"),
        name="xattn_prompt",
    )(q, mk, mv)


def _xattn_sample_kernel(q_ref, k_ref, v_ref, o_ref, *, G, T, nm):
    R = X_HEADS * T
    W = X_HEADS * HEAD_DIM
    rrow = lax.broadcasted_iota(jnp.int32, (R, W), 0)
    rcol = lax.broadcasted_iota(jnp.int32, (R, W), 1)
    own_head = (rrow // T) == (rcol // HEAD_DIM)
    for g in range(G):
        q = q_ref[g]
        qbd = jnp.where(own_head, jnp.concatenate([q] * X_HEADS, axis=0), 0.0).astype(BF16)
        s = _dot_nt(qbd, _heads_to_lanes(k_ref.at[g], X_HEADS, nm)) * ATT_SCALE
        p = jnp.exp(s - jnp.max(s, axis=-1, keepdims=True))
        p = p / jnp.sum(p, axis=-1, keepdims=True)
        of = jnp.where(own_head, _dot(p.astype(BF16), _heads_to_lanes(v_ref.at[g], X_HEADS, nm)), 0.0)
        out = of[0:T]
        for h in range(1, X_HEADS):
            out = out + of[h * T:(h + 1) * T]
        o_ref[g] = out.astype(o_ref.dtype)


def _xattn_sample(q3, mk, mv, layer, *, G):
    B, T, W = q3.shape
    nm = mk.shape[2] // X_HEADS
    return pl.pallas_call(
        functools.partial(_xattn_sample_kernel, G=G, T=T, nm=nm),
        grid=(B // G,),
        in_specs=[pl.BlockSpec((G, T, W), lambda b: (b, 0, 0)),
                  pl.BlockSpec((None, G, nm * X_HEADS, HEAD_DIM), lambda b: (layer, b, 0, 0)),
                  pl.BlockSpec((None, G, nm * X_HEADS, HEAD_DIM), lambda b: (layer, b, 0, 0))],
        out_specs=pl.BlockSpec((G, T, W), lambda b: (b, 0, 0)),
        out_shape=jax.ShapeDtypeStruct((B, T, W), BF16),
        compiler_params=_cparams("parallel"),
        name="xattn_sample",
    )(q3, mk, mv)


def _lru_kernel(y_ref, xc_ref, wr_ref, wi_ref, br_ref, bi_ref, lam_ref, h0_ref, o_ref, hl_ref,
                hc_scr, a_scr, b_scr, h_scr, *, G, Tt, nt):
    t = pl.program_id(1)
    rows = G * Tt
    ngroups = rows // SUBLANES
    tc = a_scr.shape[1]
    wb = tc // LRU_BLOCKS

    @pl.when(t == 0)
    def _():
        hc_scr[...] = h0_ref[...]

    xc = xc_ref[...].reshape(rows, tc)
    xbs = [xc[:, n * wb:(n + 1) * wb].astype(BF16) for n in range(LRU_BLOCKS)]
    gate = lambda w_ref, b_ref: _sigmoid(
        jnp.concatenate([_dot(xbs[n], w_ref[n]) for n in range(LRU_BLOCKS)], axis=1) + b_ref[...])
    r = gate(wr_ref, br_ref)
    i = gate(wi_ref, bi_ref)
    log_a = LRU_C * r * _neg_softplus(-lam_ref[...])
    a = jnp.exp(log_a)
    a_scr[...] = a
    b_scr[...] = jnp.sqrt(-jnp.tanh(log_a) * (a * a + 1.0)) * (i * xc)
    sub = lax.broadcasted_iota(jnp.int32, (SUBLANES, tc), 0)

    def group(gi, carry):
        r0 = pl.multiple_of(gi * SUBLANES, SUBLANES)
        a = a_scr[pl.ds(r0, SUBLANES), :]
        b = b_scr[pl.ds(r0, SUBLANES), :]
        for s in (1, 2, 4):
            keep = sub >= s
            a_sh = jnp.where(keep, pltpu.roll(a, s, 0), 1.0)
            b_sh = jnp.where(keep, pltpu.roll(b, s, 0), 0.0)
            b = a * b_sh + b
            a = a * a_sh
        hprev = hc_scr[gi] if Tt == SUBLANES else carry
        h8 = b + a * hprev
        h_scr[pl.ds(r0, SUBLANES), :] = h8
        last = h8[SUBLANES - 1:SUBLANES, :]
        if Tt == SUBLANES:
            hl_ref[gi] = last
            return carry
        return last

    carry = lax.fori_loop(0, ngroups, group, hc_scr[0])
    if Tt != SUBLANES:
        hc_scr[0] = carry

        @pl.when(t == nt - 1)
        def _():
            hl_ref[0] = carry

    y = y_ref[...].reshape(rows, tc)
    o_ref[...] = (jax.nn.gelu(y) * h_scr[...]).astype(o_ref.dtype).reshape(G, Tt, tc)


def _lru(proj3, xc3, w_r, w_i, b_r, b_i, lam, h0, *, G, Tt):
    B, T, D = xc3.shape
    wb = D // LRU_BLOCKS
    nt = T // Tt
    assert Tt % SUBLANES == 0 and (G == 1 or Tt == SUBLANES)
    vec = lambda a: a.astype(F32).reshape(1, D)
    whole = lambda shape: pl.BlockSpec(shape, lambda b, t: (0,) * len(shape))
    return pl.pallas_call(
        functools.partial(_lru_kernel, G=G, Tt=Tt, nt=nt),
        grid=(B // G, nt),
        in_specs=[pl.BlockSpec((G, Tt, D), lambda b, t: (b, t, 0)),
                  pl.BlockSpec((G, Tt, D), lambda b, t: (b, t, 0)),
                  whole((LRU_BLOCKS, wb, wb)), whole((LRU_BLOCKS, wb, wb)),
                  whole((1, D)), whole((1, D)), whole((1, D)),
                  pl.BlockSpec((G, 1, D), lambda b, t: (b, 0, 0))],
        out_specs=[pl.BlockSpec((G, Tt, D), lambda b, t: (b, t, 0)),
                   pl.BlockSpec((G, 1, D), lambda b, t: (b, 0, 0))],
        out_shape=[jax.ShapeDtypeStruct((B, T, D), BF16),
                   jax.ShapeDtypeStruct((B, 1, D), F32)],
        scratch_shapes=[pltpu.VMEM((G, 1, D), F32),
                        pltpu.VMEM((G * Tt, D), F32),
                        pltpu.VMEM((G * Tt, D), F32),
                        pltpu.VMEM((G * Tt, D), F32)],
        compiler_params=_cparams("parallel", "arbitrary"),
        name="rglru",
    )(proj3, xc3, w_r, w_i, vec(b_r), vec(b_i), vec(lam), h0.reshape(B, 1, D))


def _pad_buf(buf):
    return jnp.pad(buf, ((0, 0), (BUF_ROWS - buf.shape[1], 0), (0, 0)))


def _row_tile(m):
    return 512 if m % 512 == 0 else m


def _mm_row_tile(m):
    return 1024 if m % 1024 == 0 else _row_tile(m)


def _col_tile(n):
    return 1024 if n % 1024 == 0 else 512 if n % 512 == 0 else n


def _wblocks(w):
    return w.astype(BF16)


class _Stream:
    def __init__(self, B, T):
        self.B, self.T = B, T
        self.M = B * T
        self.long = T > SUBLANES
        self.G = 1 if self.long else B


def _ab_layer(st, x, w, dn_conv, dn_S, sb_fn):
    B, T, M = st.B, st.T, st.M
    tm = _mm_row_tile(M)
    proj = _mm(x, w["ab_main"], tm=tm)
    ab = _mm(x, w["ab_ab"], tm=tm)
    proj3 = proj.reshape(B, T, -1)
    nqkv = 3 * DN_HEADS * HEAD_DIM
    qkvn = _conv_dn(proj3, _pad_buf(dn_conv), w["ab_conv_w"], G=st.G)
    C = 64 if st.long else T
    o_dn, s_new = _delta(qkvn, proj3, ab.reshape(B, T, LANES), dn_S, w["dn_A_log"], w["dn_dt_bias"],
                         w["dn_norm_w"], C=C, nb=2 if B % 2 == 0 else 1)
    o_sb = sb_fn(proj)
    o = jnp.concatenate([o_dn.reshape(M, -1), o_sb], axis=-1)
    new_conv = proj3[:, T - (DN_CONV - 1):, :nqkv]
    W = SB_HEADS * HEAD_DIM
    k_new = proj[:, 5 * W:6 * W].reshape(B, T, SB_HEADS, HEAD_DIM)
    v_new = proj[:, 6 * W:7 * W].reshape(B, T, SB_HEADS, HEAD_DIM)
    return o, w["ab_w_out"], (k_new, v_new, new_conv, s_new)


def _lru_layer(st, x, w, lru_conv, lru_h):
    B, T, M = st.B, st.T, st.M
    D = x.shape[1]
    proj = _mm(x, w["c_w_in"], tm=_mm_row_tile(M))
    proj3 = proj.reshape(B, T, 2 * D)
    tc = 256
    xc3 = _conv_lin(proj3, D // tc, _pad_buf(lru_conv), w["lru_conv_w"], w["lru_conv_b"], G=st.G, tc=tc)
    G, Tt = (1, 256) if st.long else (min(32, B), T)
    o3, h_last = _lru(proj3, xc3, w["lru_w_r"], w["lru_w_i"], w["lru_b_r"], w["lru_b_i"],
                      w["lru_lambda"], lru_h, G=G, Tt=Tt)
    new_conv = proj3[:, T - (LRU_CONV - 1):, D:]
    return o3.reshape(M, D), w["c_w_out"], (new_conv, h_last.reshape(B, D))


def _xattn_layer(st, x, w, mk3, mv3, layer):
    B, T, M = st.B, st.T, st.M
    q = _mm(x, w["xa_w_q"], tm=_mm_row_tile(M))
    if st.long:
        return _xattn_prompt(q, mk3, mv3, B=B, T=T, tq=512)
    W = q.shape[1]
    return _xattn_sample(q.reshape(B, T, W), mk3, mv3, layer, G=8).reshape(M, W)


def _ffn_layer(st, x, w, ffn_conv, d_ff, g, b):
    B, T, M = st.B, st.T, st.M
    ffp = w["ffn_w_down"].shape[0]
    halves = lambda a: (a[..., :d_ff], a[..., d_ff:])
    padc = lambda a: jnp.pad(a, [(0, 0)] * (a.ndim - 1) + [(0, ffp - d_ff)])
    hist = tuple(_pad_buf(padc(h)) for h in halves(ffn_conv))
    y, tail_g, tail_v = _ffn(x, w["ffn_w_up"], w["ffn_conv_w"], w["ffn_conv_b"], hist,
                             w["ffn_w_down"], g, b, B=B, T=T, tm=_row_tile(M), tf=FFN_COL_TILE)
    keep = slice(BUF_ROWS - (FFN_CONV - 1), BUF_ROWS)
    new_conv = jnp.concatenate([tail_g[:, keep, :d_ff], tail_v[:, keep, :d_ff]], axis=-1)
    return y, new_conv


def kernel(x_prompt, x_sample, mem_prompt, cache_sb_k, cache_sb_v, cache_mem_k, cache_mem_v, state_dn_conv, state_dn_S, state_lru_conv, state_lru_h, state_ffn_conv, page_table, ab_w_in, ab_conv_w, dn_A_log, dn_dt_bias, dn_norm_w, sb_bias, ab_w_out, c_w_in, lru_conv_w, lru_conv_b, lru_w_r, lru_b_r, lru_w_i, lru_b_i, lru_lambda, c_w_out, xa_w_q, xa_w_k, xa_w_v, xa_w_o, ffn_w_up, ffn_conv_w, ffn_conv_b, ffn_w_down, ln_g, ln_b):
    bp, tp, d_model = x_prompt.shape
    bs, ts, _ = x_sample.shape
    sp, ss = _Stream(bp, tp), _Stream(bs, ts)
    dt = x_prompt.dtype
    n_mem = mem_prompt.shape[1]
    xw = X_HEADS * HEAD_DIM
    nqkv = 3 * DN_HEADS * HEAD_DIM
    d_ff = ffn_w_down.shape[1]
    ffp = -(-d_ff // FFN_COL_TILE) * FFN_COL_TILE
    depth = ln_g.shape[0]

    xp = x_prompt.reshape(sp.M, d_model)
    xs = x_sample.reshape(ss.M, d_model)
    mem2 = mem_prompt.reshape(bp * n_mem, d_model)

    outs_p = {k: [] for k in ("sb_k", "sb_v", "dn_conv", "dn_S", "lru_conv", "lru_h", "mem_k", "mem_v", "ffn_conv")}
    outs_s = {k: [] for k in ("sb_k", "sb_v", "dn_conv", "dn_S", "lru_conv", "lru_h", "ffn_conv")}

    def pad_ff(a, axis):
        pad = [(0, 0)] * a.ndim
        pad[axis] = (0, ffp - d_ff)
        return jnp.pad(a, pad)

    for layer in range(depth):
        w = {}
        if layer % 2 == 0:
            e = layer // 2
            w_in = ab_w_in[e]
            c_ab = nqkv + DN_HEADS * HEAD_DIM
            w["ab_main"] = _wblocks(jnp.concatenate([w_in[:, :c_ab], w_in[:, c_ab + 2 * DN_HEADS:]], axis=1))
            w["ab_ab"] = _wblocks(jnp.pad(w_in[:, c_ab:c_ab + 2 * DN_HEADS], ((0, 0), (0, LANES - 2 * DN_HEADS))))
            w["ab_conv_w"] = ab_conv_w[e]
            w["dn_A_log"], w["dn_dt_bias"], w["dn_norm_w"] = dn_A_log[e], dn_dt_bias[e], dn_norm_w[e]
            w["ab_w_out"] = ab_w_out[e].astype(BF16)
        else:
            o = layer // 2
            w["c_w_in"] = _wblocks(c_w_in[o])
            w["lru_conv_w"], w["lru_conv_b"] = lru_conv_w[o], lru_conv_b[o]
            w["lru_w_r"], w["lru_w_i"] = lru_w_r[o].astype(BF16), lru_w_i[o].astype(BF16)
            w["lru_b_r"], w["lru_b_i"], w["lru_lambda"] = lru_b_r[o], lru_b_i[o], lru_lambda[o]
            w["c_w_out"] = c_w_out[o].astype(BF16)
        w["xa_w_q"] = _wblocks(xa_w_q[layer])
        w["ffn_w_up"] = (pad_ff(ffn_w_up[layer][:, :d_ff].astype(BF16), 1),
                         pad_ff(ffn_w_up[layer][:, d_ff:].astype(BF16), 1))
        w["ffn_conv_w"] = (pad_ff(ffn_conv_w[layer][:, :d_ff], 1), pad_ff(ffn_conv_w[layer][:, d_ff:], 1))
        w["ffn_conv_b"] = (pad_ff(ffn_conv_b[layer][:d_ff], 0), pad_ff(ffn_conv_b[layer][d_ff:], 0))
        w["ffn_w_down"] = pad_ff(ffn_w_down[layer].astype(BF16), 0)
        w_xo = xa_w_o[layer].astype(BF16)

        if layer % 2 == 0:
            e = layer // 2
            qcol, kcol, vcol = 4 * DN_HEADS, 5 * DN_HEADS, 6 * DN_HEADS
            sbp = functools.partial(_sb_prompt, sb_bias=sb_bias[e], B=bp, T=tp, qcol=qcol, kcol=kcol, vcol=vcol, tq=256)
            op, wout, (k_new, v_new, cb, s_new) = _ab_layer(
                sp, xp, w, jnp.zeros((bp, DN_CONV - 1, nqkv), dt),
                jnp.zeros((bp, DN_HEADS, HEAD_DIM, HEAD_DIM), F32), sbp)
            outs_p["sb_k"].append(k_new); outs_p["sb_v"].append(v_new)
            outs_p["dn_conv"].append(cb); outs_p["dn_S"].append(s_new)
            n_pool = cache_sb_k.shape[1]
            ck = cache_sb_k[e].reshape(n_pool, PAGE_SIZE * SB_HEADS, HEAD_DIM)
            cv = cache_sb_v[e].reshape(n_pool, PAGE_SIZE * SB_HEADS, HEAD_DIM)
            sbs = functools.partial(_sb_sample, cache_k=ck, cache_v=cv, page_table=page_table, sb_bias=sb_bias[e],
                                    B=bs, T=ts, qcol=4, kcol=5, vcol=6)
            os_, _, (k_new, v_new, cb, s_new) = _ab_layer(ss, xs, w, state_dn_conv[e], state_dn_S[e], sbs)
            outs_s["sb_k"].append(k_new); outs_s["sb_v"].append(v_new)
            outs_s["dn_conv"].append(cb); outs_s["dn_S"].append(s_new)
        else:
            o = layer // 2
            op, wout, (cb, hl) = _lru_layer(sp, xp, w, jnp.zeros((bp, LRU_CONV - 1, d_model), dt),
                                            jnp.zeros((bp, d_model), dt))
            outs_p["lru_conv"].append(cb); outs_p["lru_h"].append(hl)
            os_, _, (cb, hl) = _lru_layer(ss, xs, w, state_lru_conv[o], state_lru_h[o])
            outs_s["lru_conv"].append(cb); outs_s["lru_h"].append(hl)
        xp = _mm_ln(op, wout, xp, ln_g[layer, 0], ln_b[layer, 0], tm=LN_ROW_TILE // 2, tk=wout.shape[0])
        xs = _mm_ln(os_, wout, xs, ln_g[layer, 0], ln_b[layer, 0], tm=LN_ROW_TILE // 2, tk=wout.shape[0])

        mk = _mm(mem2, _wblocks(xa_w_k[layer]), tm=_row_tile(bp * n_mem))
        mv = _mm(mem2, _wblocks(xa_w_v[layer]), tm=_row_tile(bp * n_mem))
        outs_p["mem_k"].append(mk.reshape(bp, n_mem, X_HEADS, HEAD_DIM))
        outs_p["mem_v"].append(mv.reshape(bp, n_mem, X_HEADS, HEAD_DIM))
        ap = _xattn_layer(sp, xp, w, mk.reshape(bp, n_mem, xw), mv.reshape(bp, n_mem, xw), layer)
        as_ = _xattn_layer(ss, xs, w, cache_mem_k.reshape(depth, bs, n_mem * X_HEADS, HEAD_DIM),
                           cache_mem_v.reshape(depth, bs, n_mem * X_HEADS, HEAD_DIM), layer)
        xp = _mm_ln(ap, w_xo, xp, ln_g[layer, 1], ln_b[layer, 1], tm=LN_ROW_TILE, tk=xw)
        xs = _mm_ln(as_, w_xo, xs, ln_g[layer, 1], ln_b[layer, 1], tm=LN_ROW_TILE, tk=xw)

        xp, fb = _ffn_layer(sp, xp, w, jnp.zeros((bp, FFN_CONV - 1, 2 * d_ff), dt), d_ff,
                            ln_g[layer, 2], ln_b[layer, 2])
        outs_p["ffn_conv"].append(fb)
        xs, fb = _ffn_layer(ss, xs, w, state_ffn_conv[layer], d_ff, ln_g[layer, 2], ln_b[layer, 2])
        outs_s["ffn_conv"].append(fb)

    st = lambda xs_: jnp.stack(xs_)
    return (xp.reshape(bp, tp, d_model), xs.reshape(bs, ts, d_model),
            st(outs_p["sb_k"]), st(outs_p["sb_v"]), st(outs_p["dn_conv"]), st(outs_p["dn_S"]),
            st(outs_p["lru_conv"]), st(outs_p["lru_h"]), st(outs_p["mem_k"]), st(outs_p["mem_v"]),
            st(outs_p["ffn_conv"]),
            st(outs_s["sb_k"]), st(outs_s["sb_v"]), st(outs_s["dn_conv"]), st(outs_s["dn_S"]),
            st(outs_s["lru_conv"]), st(outs_s["lru_h"]), st(outs_s["ffn_conv"]))
```
